```python
import math
import jax
import jax.numpy as jnp
from jax import lax
import numpy as np

D_MODEL = 1024
BATCH = 1
SEQ = 16384
DEPTH = 4
DEC_BATCH = 8
DEC_SEQ = 4096
PAST_LEN = 128

N_MIXERS = 4
GRID_W = 64
RMS_EPS = 1e-6

NA_HEADS = 16
NA_HEAD_DIM = D_MODEL // NA_HEADS
NA_WIN_ROWS = 8
NA_WIN_COLS = 16

LRU_WIDTH = D_MODEL
LRU_BLOCKS = 16
LRU_BLOCK = LRU_WIDTH // LRU_BLOCKS
LRU_C = 8.0
LRU_CONV = 4

GLA_HEADS = 4
GLA_DK = D_MODEL // 2
GLA_DV = D_MODEL
GLA_HK = GLA_DK // GLA_HEADS
GLA_HV = GLA_DV // GLA_HEADS
GLA_GATE_RANK = 16
GLA_TAU = 16.0
GLA_CHUNK = 64

HY_ORDER = 2
HY_SHORT = 3
HY_BANDS = 16
HY_EMB = 1 + 2 * HY_BANDS
HY_FILTER_HID = 64
HY_MIN_DECAY = 3.07
HY_MAX_DECAY = 15.35

MEM_TOKENS = 256
XA_HEADS = 4
XA_HEAD_DIM = D_MODEL // XA_HEADS

D_FF = 3584
N_EXPERTS = 8
TOP_K = 2

kernel_name = "hybrid_bidir_na_lru_gla_hyena_encoder"

F32 = jnp.float32


def _rmsnorm(x, g):
    xf = x.astype(F32)
    y = xf * lax.rsqrt(jnp.mean(xf * xf, axis=-1, keepdims=True) + RMS_EPS)
    return (y * g.astype(F32)).astype(x.dtype)


def _short_conv(x, w, b, left):
    K = w.shape[0]
    L = x.shape[1]
    xp = jnp.pad(x, ((0, 0), (left, K - 1 - left), (0, 0)))
    y = b
    for k in range(K):
        y = y + xp[:, k:k + L] * w[k]
    return y


def _na2d_mixer(x, w_qkv, rpb, w_out):
    B, L, _ = x.shape
    rows = L // GRID_W
    win_r = min(NA_WIN_ROWS, rows)
    qkv = (x @ w_qkv).reshape(B, rows, GRID_W, 3, NA_HEADS, NA_HEAD_DIM)
    q = qkv[:, :, :, 0] * (NA_HEAD_DIM ** -0.5)
    k = qkv[:, :, :, 1]
    v = qkv[:, :, :, 2]
    row_start = jnp.clip(jnp.arange(rows) - win_r // 2, 0, rows - win_r)
    col_start = jnp.clip(jnp.arange(GRID_W) - NA_WIN_COLS // 2, 0, GRID_W - NA_WIN_COLS)
    col_idx = col_start[:, None] + jnp.arange(NA_WIN_COLS)[None, :]
    col_rel = col_idx - jnp.arange(GRID_W)[:, None] + (NA_WIN_COLS - 1)

    def one_row(r):
        rs = row_start[r]
        k_g = lax.dynamic_slice_in_dim(k, rs, win_r, axis=1)[:, :, col_idx]
        v_g = lax.dynamic_slice_in_dim(v, rs, win_r, axis=1)[:, :, col_idx]
        q_r = lax.dynamic_index_in_dim(q, r, axis=1, keepdims=False)
        s = jnp.einsum('bwhd,bjwnhd->bhwjn', q_r, k_g).astype(F32)
        row_rel = rs + jnp.arange(win_r) - r + (NA_WIN_ROWS - 1)
        bias = rpb[:, row_rel[None, :, None], col_rel[:, None, :]].astype(F32)
        s = s + bias[None]
        p = jax.nn.softmax(s.reshape(B, NA_HEADS, GRID_W, win_r * NA_WIN_COLS), axis=-1)
        p = p.reshape(s.shape).astype(v.dtype)
        return jnp.einsum('bhwjn,bjwnhd->bwhd', p, v_g)

    o = lax.map(one_row, jnp.arange(rows))
    o = jnp.moveaxis(o, 0, 1).reshape(B, L, D_MODEL)
    return o @ w_out


def _lru_combine(lhs, rhs):
    a1, b1 = lhs
    a2, b2 = rhs
    return a1 * a2, a2 * b1 + b2


def _rglru(xc, wa, ba, wx, bx, lam, reverse):
    B, L, W = xc.shape
    xb = xc.reshape(B, L, LRU_BLOCKS, LRU_BLOCK)
    gate_a = jnp.einsum('blnd,nde->blne', xb, wa).reshape(B, L, W) + ba
    gate_x = jnp.einsum('blnd,nde->blne', xb, wx).reshape(B, L, W) + bx
    r = jax.nn.sigmoid(gate_a.astype(F32))
    i = jax.nn.sigmoid(gate_x.astype(F32))
    log_a = -LRU_C * r * jax.nn.softplus(-lam.astype(F32))
    a = jnp.exp(log_a)
    b = jnp.sqrt(-jnp.expm1(2.0 * log_a)) * (i * xc.astype(F32))
    _, h = lax.associative_scan(_lru_combine, (a, b), axis=1, reverse=reverse)
    return h


def _rglru_mixer(x, w_in, conv_w, conv_b, ga_w, ga_b, gx_w, gx_b, lam, w_out):
    gate, branch = jnp.split(x @ w_in, 2, axis=-1)
    xc = _short_conv(branch, conv_w, conv_b, LRU_CONV // 2)
    h = (_rglru(xc, ga_w[0], ga_b[0], gx_w[0], gx_b[0], lam[0], False)
         + _rglru(xc, ga_w[1], ga_b[1], gx_w[1], gx_b[1], lam[1], True))
    y = jax.nn.gelu(gate.astype(F32)) * h
    return y @ w_out


def _gla_direction(q, k, v, g, strict):
    B, H, L, dk = q.shape
    dv = v.shape[-1]
    n = L // GLA_CHUNK
    q, k, v, g = (t.reshape(B, H, n, GLA_CHUNK, t.shape[-1]) for t in (q, k, v, g))
    b = jnp.cumsum(g, axis=3)
    b_last = b[:, :, :, -1:]
    q_b = q * jnp.exp(b)
    k_b = k * jnp.exp(-b)
    mask = jnp.tril(jnp.ones((GLA_CHUNK, GLA_CHUNK), dtype=bool), -1 if strict else 0)
    att = jnp.where(mask, jnp.einsum('bhnid,bhnjd->bhnij', q_b, k_b), 0.0)
    o = jnp.einsum('bhnij,bhnje->bhnie', att, v)
    u = jnp.einsum('bhncd,bhnce->bhnde', k * jnp.exp(b_last - b), v)
    decay = jnp.exp(b_last[:, :, :, 0])

    def step(S, inp):
        dec, u_n = inp
        return dec[..., None] * S + u_n, S

    s0 = jnp.zeros((B, H, dk, dv), F32)
    _, s_prev = lax.scan(step, s0, (jnp.moveaxis(decay, 2, 0), jnp.moveaxis(u, 2, 0)))
    o = o + jnp.einsum('bhncd,nbhde->bhnce', q_b, s_prev)
    return o.reshape(B, H, L, dv)


def _gla_mixer(x, w_in, gate_w1, gate_w2, gate_b, head_norm, w_out):
    B, L, _ = x.shape
    q, k, v, r = jnp.split(x @ w_in, [GLA_DK, 2 * GLA_DK, 2 * GLA_DK + GLA_DV], axis=-1)

    def heads(t):
        return t.reshape(B, L, GLA_HEADS, -1).transpose(0, 2, 1, 3).astype(F32)

    def log_gate(d):
        z = (x @ gate_w1[d]) @ gate_w2[d] + gate_b[d]
        return heads(jax.nn.log_sigmoid(z.astype(F32)) / GLA_TAU)

    def flip(t):
        return jnp.flip(t, axis=2)

    q = heads(q) * (GLA_HK ** -0.5)
    k = heads(k)
    v = heads(v)
    o = _gla_direction(q, k, v, log_gate(0), False)
    o = o + flip(_gla_direction(flip(q), flip(k), flip(v), flip(log_gate(1)), True))
    o = _rmsnorm(o, head_norm)
    o = o.transpose(0, 2, 1, 3).reshape(B, L, GLA_DV)
    y = o * jax.nn.silu(r.astype(F32))
    return y @ w_out


def _hyena_filters(L, w1, b1, w2, b2, w3, freq, log_decay):
    j = jnp.arange(L, dtype=F32)
    t = j / L
    bands = jnp.linspace(1e-4, HY_BANDS - 1, HY_BANDS, dtype=F32)
    ang = (2.0 * math.pi / L) * j[:, None] * bands[None, :]
    feats = jnp.concatenate([t[:, None], jnp.cos(ang), jnp.sin(-ang)], axis=-1)
    freq = freq.astype(F32)
    h = jnp.sin(freq[0] * (feats @ w1.astype(F32) + b1.astype(F32)))
    h = jnp.sin(freq[1] * (h @ w2.astype(F32) + b2.astype(F32)))
    h = (h @ w3.astype(F32)).reshape(L, HY_ORDER, D_MODEL)
    dist = jnp.abs(j - L // 2) / (L / 2)
    h = h * jnp.exp(-dist[:, None, None] * jnp.exp(log_decay.astype(F32))[None])
    return h / jnp.sum(jnp.abs(h), axis=0, keepdims=True)


def _fft_conv_centred(z, h):
    L = z.shape[1]
    n = 2 * L
    zf = jnp.fft.rfft(z, n=n, axis=1)
    hf = jnp.fft.rfft(h, n=n, axis=0)
    y = jnp.fft.irfft(zf * hf[None], n=n, axis=1)
    return y[:, L // 2: L // 2 + L]


def _hyena_mixer(x, w_in, short_w, short_b, fw1, fb1, fw2, fb2, fw3, ffreq, log_decay, skip, w_out):
    L = x.shape[1]
    u = _short_conv(x @ w_in, short_w, short_b, HY_SHORT // 2).astype(F32)
    v, x1, x2 = jnp.split(u, 3, axis=-1)
    filt = _hyena_filters(L, fw1, fb1, fw2, fb2, fw3, ffreq, log_decay)
    skip = skip.astype(F32)
    z = v
    for o, gate in enumerate((x1, x2)):
        z = gate * (_fft_conv_centred(z, filt[:, o]) + skip[o] * z)
    return z @ w_out


def _memory_xattn(x, m, w_q, w_kv, w_out):
    B, L, _ = x.shape
    M = m.shape[1]
    q = (x @ w_q).reshape(B, L, XA_HEADS, XA_HEAD_DIM)
    kv = (m @ w_kv).reshape(B, M, 2, XA_HEADS, XA_HEAD_DIM)
    k = kv[:, :, 0]
    v = kv[:, :, 1]
    s = jnp.einsum('blhd,bmhd->bhlm', q, k).astype(F32) * (XA_HEAD_DIM ** -0.5)
    p = jax.nn.softmax(s, axis=-1).astype(v.dtype)
    o = jnp.einsum('bhlm,bmhd->blhd', p, v).reshape(B, L, D_MODEL)
    return o @ w_out


def _swiglu(x, w_gate_up, w_down):
    g, u = jnp.split(x @ w_gate_up, 2, axis=-1)
    return (jax.nn.silu(g) * u) @ w_down


def _moe(x, w_router, w_gate_up, w_down):
    B, L, D = x.shape
    xt = x.reshape(B * L, D)
    logits = (xt @ w_router).astype(F32)
    top_val, top_idx = lax.top_k(logits, TOP_K)
    top_w = jax.nn.softmax(top_val, axis=-1)
    comb = jnp.einsum('nk,nke->ne', top_w, jax.nn.one_hot(top_idx, N_EXPERTS, dtype=F32))
    out = jnp.zeros((B * L, D), F32)
    for e in range(N_EXPERTS):
        out = out + comb[:, e:e + 1] * _swiglu(xt, w_gate_up[e], w_down[e])
    return out.reshape(B, L, D)


def _trunk(x, mem,
           norm_mix, norm_xattn, norm_mem, norm_ffn, norm_final,
           na_w_qkv, na_rpb, na_w_out,
           lru_w_in, lru_conv_w, lru_conv_b, lru_gate_a_w, lru_gate_a_b, lru_gate_x_w, lru_gate_x_b, lru_lambda, lru_w_out,
           gla_w_in, gla_gate_w1, gla_gate_w2, gla_gate_b, gla_head_norm, gla_w_out,
           hy_w_in, hy_short_w, hy_short_b, hy_filt_w1, hy_filt_b1, hy_filt_w2, hy_filt_b2, hy_filt_w3, hy_filt_freq, hy_log_decay, hy_skip, hy_w_out,
           xa_w_q, xa_w_kv, xa_w_out,
           ffn_w_gate_up, ffn_w_down,
           moe_router, moe_w_gate_up, moe_w_down):
    for i in range(DEPTH):
        m, j = i % N_MIXERS, i // N_MIXERS
        h = _rmsnorm(x, norm_mix[i])
        if m == 0:
            y = _na2d_mixer(h, na_w_qkv[j], na_rpb[j], na_w_out[j])
        elif m == 1:
            y = _rglru_mixer(h, lru_w_in[j], lru_conv_w[j], lru_conv_b[j], lru_gate_a_w[j], lru_gate_a_b[j],
                             lru_gate_x_w[j], lru_gate_x_b[j], lru_lambda[j], lru_w_out[j])
        elif m == 2:
            y = _gla_mixer(h, gla_w_in[j], gla_gate_w1[j], gla_gate_w2[j], gla_gate_b[j], gla_head_norm[j], gla_w_out[j])
        else:
            y = _hyena_mixer(h, hy_w_in[j], hy_short_w[j], hy_short_b[j], hy_filt_w1[j], hy_filt_b1[j], hy_filt_w2[j],
                             hy_filt_b2[j], hy_filt_w3[j], hy_filt_freq[j], hy_log_decay[j], hy_skip[j], hy_w_out[j])
        x = x + y.astype(x.dtype)
        y = _memory_xattn(_rmsnorm(x, norm_xattn[i]), _rmsnorm(mem, norm_mem[i]), xa_w_q[i], xa_w_kv[i], xa_w_out[i])
        x = x + y.astype(x.dtype)
        h = _rmsnorm(x, norm_ffn[i])
        if i % 2 == 0:
            y = _swiglu(h, ffn_w_gate_up[i // 2], ffn_w_down[i // 2])
        else:
            y = _moe(h, moe_router[i // 2], moe_w_gate_up[i // 2], moe_w_down[i // 2])
        x = x + y.astype(x.dtype)
    return _rmsnorm(x, norm_final)


def setup_inputs(seed: int = 0) -> dict:
    key = jax.random.key(seed)
    keys = iter(jax.random.split(key, 64))

    def nrm(shape, scale):
        return jax.random.normal(next(keys), shape, F32) * scale

    def gain(shape):
        return 1.0 + nrm(shape, 0.02)

    n_a, n_b, n_c, n_d = (len(range(m, DEPTH, N_MIXERS)) for m in range(N_MIXERS))
    n_dense = len(range(0, DEPTH, 2))
    n_moe = len(range(1, DEPTH, 2))
    D = D_MODEL

    lam_u = jax.random.uniform(next(keys), (n_b, 2, LRU_WIDTH), F32, 0.9, 0.999) ** (1.0 / LRU_C)
    hy_log_decay = jnp.log(jax.random.uniform(next(keys), (n_d, 2, D), F32, HY_MIN_DECAY, HY_MAX_DECAY))

    return {
        'x_prompt': nrm((BATCH, SEQ, D), 1.0),
        'x_sample': nrm((DEC_BATCH, DEC_SEQ, D), 1.0),
        'mem_prompt': nrm((BATCH, MEM_TOKENS, D), 1.0),
        'mem_sample': nrm((DEC_BATCH, MEM_TOKENS, D), 1.0),
        'norm_mix': gain((DEPTH, D)),
        'norm_xattn': gain((DEPTH, D)),
        'norm_mem': gain((DEPTH, D)),
        'norm_ffn': gain((DEPTH, D)),
        'norm_final': gain((D,)),
        'na_w_qkv': nrm((n_a, D, 3 * D), D ** -0.5),
        'na_rpb': nrm((n_a, NA_HEADS, 2 * NA_WIN_ROWS - 1, 2 * NA_WIN_COLS - 1), 0.1),
        'na_w_out': nrm((n_a, D, D), D ** -0.5),
        'lru_w_in': nrm((n_b, D, 2 * LRU_WIDTH), D ** -0.5),
        'lru_conv_w': nrm((n_b, LRU_CONV, LRU_WIDTH), LRU_CONV ** -0.5),
        'lru_conv_b': nrm((n_b, LRU_WIDTH), 0.02),
        'lru_gate_a_w': nrm((n_b, 2, LRU_BLOCKS, LRU_BLOCK, LRU_BLOCK), LRU_BLOCK ** -0.5),
        'lru_gate_a_b': nrm((n_b, 2, LRU_WIDTH), 0.02),
        'lru_gate_x_w': nrm((n_b, 2, LRU_BLOCKS, LRU_BLOCK, LRU_BLOCK), LRU_BLOCK ** -0.5),
        'lru_gate_x_b': nrm((n_b, 2, LRU_WIDTH), 0.02),
        'lru_lambda': jnp.log(lam_u) - jnp.log1p(-lam_u),
        'lru_w_out': nrm((n_b, LRU_WIDTH, D), LRU_WIDTH ** -0.5),
        'gla_w_in': nrm((n_c, D, 2 * GLA_DK + 2 * GLA_DV), D ** -0.5),
        'gla_gate_w1': nrm((n_c, 2, D, GLA_GATE_RANK), D ** -0.5),
        'gla_gate_w2': nrm((n_c, 2, GLA_GATE_RANK, GLA_DK), GLA_GATE_RANK ** -0.5),
        'gla_gate_b': nrm((n_c, 2, GLA_DK), 0.1),
        'gla_head_norm': gain((n_c, GLA_HV)),
        'gla_w_out': nrm((n_c, GLA_DV, D), GLA_DV ** -0.5),
        'hy_w_in': nrm((n_d, D, 3 * D), D ** -0.5),
        'hy_short_w': nrm((n_d, HY_SHORT, 3 * D), HY_SHORT ** -0.5),
        'hy_short_b': nrm((n_d, 3 * D), 0.02),
        'hy_filt_w1': nrm((n_d, HY_EMB, HY_FILTER_HID), HY_EMB ** -0.5),
        'hy_filt_b1': nrm((n_d, HY_FILTER_HID), 0.1),
        'hy_filt_w2': nrm((n_d, HY_FILTER_HID, HY_FILTER_HID), HY_FILTER_HID ** -0.5),
        'hy_filt_b2': nrm((n_d, HY_FILTER_HID), 0.1),
        'hy_filt_w3': nrm((n_d, HY_FILTER_HID, HY_ORDER * D), HY_FILTER_HID ** -0.5),
        'hy_filt_freq': 1.0 + nrm((n_d, 2, HY_FILTER_HID), 0.1),
        'hy_log_decay': hy_log_decay,
        'hy_skip': nrm((n_d, HY_ORDER, D), 0.5),
        'hy_w_out': nrm((n_d, D, D), D ** -0.5),
        'xa_w_q': nrm((DEPTH, D, D), D ** -0.5),
        'xa_w_kv': nrm((DEPTH, D, 2 * D), D ** -0.5),
        'xa_w_out': nrm((DEPTH, D, D), D ** -0.5),
        'ffn_w_gate_up': nrm((n_dense, D, 2 * D_FF), D ** -0.5),
        'ffn_w_down': nrm((n_dense, D_FF, D), D_FF ** -0.5),
        'moe_router': nrm((n_moe, D, N_EXPERTS), D ** -0.5),
        'moe_w_gate_up': nrm((n_moe, N_EXPERTS, D, 2 * D_FF), D ** -0.5),
        'moe_w_down': nrm((n_moe, N_EXPERTS, D_FF, D), D_FF ** -0.5),
    }


def reference(x_prompt, x_sample, mem_prompt, mem_sample,
              norm_mix, norm_xattn, norm_mem, norm_ffn, norm_final,
              na_w_qkv, na_rpb, na_w_out,
              lru_w_in, lru_conv_w, lru_conv_b, lru_gate_a_w, lru_gate_a_b, lru_gate_x_w, lru_gate_x_b, lru_lambda, lru_w_out,
              gla_w_in, gla_gate_w1, gla_gate_w2, gla_gate_b, gla_head_norm, gla_w_out,
              hy_w_in, hy_short_w, hy_short_b, hy_filt_w1, hy_filt_b1, hy_filt_w2, hy_filt_b2, hy_filt_w3, hy_filt_freq, hy_log_decay, hy_skip, hy_w_out,
              xa_w_q, xa_w_kv, xa_w_out,
              ffn_w_gate_up, ffn_w_down,
              moe_router, moe_w_gate_up, moe_w_down):
    weights = (norm_mix, norm_xattn, norm_mem, norm_ffn, norm_final,
               na_w_qkv, na_rpb, na_w_out,
               lru_w_in, lru_conv_w, lru_conv_b, lru_gate_a_w, lru_gate_a_b, lru_gate_x_w, lru_gate_x_b, lru_lambda, lru_w_out,
               gla_w_in, gla_gate_w1, gla_gate_w2, gla_gate_b, gla_head_norm, gla_w_out,
               hy_w_in, hy_short_w, hy_short_b, hy_filt_w1, hy_filt_b1, hy_filt_w2, hy_filt_b2, hy_filt_w3, hy_filt_freq, hy_log_decay, hy_skip, hy_w_out,
               xa_w_q, xa_w_kv, xa_w_out,
               ffn_w_gate_up, ffn_w_down,
               moe_router, moe_w_gate_up, moe_w_down)
    y_prompt = _trunk(x_prompt, mem_prompt, *weights)
    y_sample = _trunk(x_sample, mem_sample, *weights)
    return (y_prompt, y_sample)
```

```python
import functools
import math

import jax
import jax.numpy as jnp
from jax import lax
from jax.experimental import pallas as pl
from jax.experimental.pallas import tpu as pltpu

F32 = jnp.float32
BF16 = jnp.bfloat16

D_MODEL = 1024
RMS_EPS = 1e-6
GRID_W = 64
NA_HEADS = 16
NA_HEAD_DIM = 64
NA_WIN_ROWS = 8
NA_WIN_COLS = 16
LRU_C = 8.0
LRU_CONV = 4
GLA_HEADS = 4
GLA_DK = 512
GLA_DV = 1024
GLA_HK = 128
GLA_HV = 256
GLA_GATE_RANK = 16
GLA_TAU = 16.0
GLA_CHUNK = 64
HY_BANDS = 16
HY_FILTER_HID = 64
MEM_TOKENS = 256
XA_HEADS = 4
XA_HEAD_DIM = 256
D_FF = 3584
N_EXPERTS = 8
NEG_BIG = -1e30

V7X_VMEM_BYTES = 64 * 1024 * 1024
VMEM_LIMIT = V7X_VMEM_BYTES - 8 * 1024 * 1024
BF16_SUBLANES = 16


def _cp(*sem):
    return pltpu.CompilerParams(dimension_semantics=sem, vmem_limit_bytes=VMEM_LIMIT)


def _pick(n, cands):
    for c in cands:
        if n % c == 0:
            return c
    return n


def _rms(x, g):
    return x * lax.rsqrt(jnp.mean(x * x, axis=-1, keepdims=True) + RMS_EPS) * g


def _sigmoid(x):
    return 1.0 / (1.0 + jnp.exp(-x))


def _softplus(x):
    return jnp.maximum(x, 0.0) + jnp.log1p(jnp.exp(-jnp.abs(x)))


def _mm(a, b):
    return jnp.dot(a, b, preferred_element_type=F32)


def _mm_nt(a, b):
    return lax.dot_general(a, b, (((1,), (1,)), ((), ())), preferred_element_type=F32)


def _mm_tn(a, b):
    return lax.dot_general(a, b, (((0,), (0,)), ((), ())), preferred_element_type=F32)


def _norm_matmul_kernel(x_ref, g_ref, w_ref, o_ref, xn_ref):
    @pl.when(pl.program_id(1) == 0)
    def _():
        xn_ref[...] = _rms(x_ref[...], g_ref[...]).astype(BF16)

    o_ref[...] = _mm(xn_ref[...], w_ref[...]).astype(o_ref.dtype)


def _norm_matmul(x, g, w, out_dtype=BF16):
    n, d = x.shape
    f = w.shape[1]
    tm = _pick(n, (1024, 512, 256))
    tn = _pick(f, (1024, 640, 512, 256, 128))
    return pl.pallas_call(
        _norm_matmul_kernel,
        grid=(n // tm, f // tn),
        in_specs=[pl.BlockSpec((tm, d), lambda i, j: (i, 0)),
                  pl.BlockSpec((1, d), lambda i, j: (0, 0)),
                  pl.BlockSpec((d, tn), lambda i, j: (0, j))],
        out_specs=pl.BlockSpec((tm, tn), lambda i, j: (i, j)),
        out_shape=jax.ShapeDtypeStruct((n, f), out_dtype),
        scratch_shapes=[pltpu.VMEM((tm, d), BF16)],
        compiler_params=_cp("parallel", "arbitrary"),
        name="norm_matmul",
    )(x, g.reshape(1, d), w)


def _final_norm_kernel(x_ref, g_ref, o_ref):
    o_ref[...] = _rms(x_ref[...], g_ref[...])


def _final_norm(x, g):
    n, d = x.shape
    tm = _pick(n, (1024, 512, 256))
    return pl.pallas_call(
        _final_norm_kernel,
        grid=(n // tm,),
        in_specs=[pl.BlockSpec((tm, d), lambda i: (i, 0)), pl.BlockSpec((1, d), lambda i: (0, 0))],
        out_specs=pl.BlockSpec((tm, d), lambda i: (i, 0)),
        out_shape=jax.ShapeDtypeStruct((n, d), F32),
        compiler_params=_cp("parallel"),
        name="final_norm",
    )(x, g.reshape(1, d))


def _proj_res(prologue, tok_inputs, row_inputs, w, res, name):
    n, d = res.shape
    k = w.shape[0]
    tm = _pick(n, (512, 256))
    n_tok, n_row = len(tok_inputs), len(row_inputs)

    def kern(*refs):
        toks = []
        for r, (_, _, _, lead) in zip(refs[:n_tok], tok_inputs):
            toks.append(r[0] if lead is not None else r[...])
        rows = [r[...] for r in refs[n_tok:n_tok + n_row]]
        w_ref, res_ref, o_ref = refs[n_tok + n_row:]
        a = prologue(*toks, *rows)
        o_ref[...] = res_ref[...] + _mm(a.astype(BF16), w_ref[...])

    in_specs, args = [], []
    for arr, width, col, lead in tok_inputs:
        if lead is None:
            in_specs.append(pl.BlockSpec((tm, width), lambda i, c=col: (i, c)))
        else:
            in_specs.append(pl.BlockSpec((1, tm, width), lambda i, c=col, l=lead: (l, i, c)))
        args.append(arr)
    for arr in row_inputs:
        in_specs.append(pl.BlockSpec(arr.shape, lambda i: (0, 0)))
        args.append(arr)
    in_specs += [pl.BlockSpec((k, d), lambda i: (0, 0)), pl.BlockSpec((tm, d), lambda i: (i, 0))]
    args += [w, res]
    return pl.pallas_call(
        kern,
        grid=(n // tm,),
        in_specs=in_specs,
        out_specs=pl.BlockSpec((tm, d), lambda i: (i, 0)),
        out_shape=jax.ShapeDtypeStruct((n, d), F32),
        compiler_params=_cp("parallel"),
        name=name,
    )(*args)


def _xattn_kernel(x_ref, g_ref, wq_ref, k_ref, v_ref, wo_ref, o_ref):
    x = x_ref[...]
    xn = _rms(x, g_ref[...]).astype(BF16)
    q = (_mm(xn, wq_ref[...]) * (XA_HEAD_DIM ** -0.5)).astype(BF16)
    outs = []
    for h in range(XA_HEADS):
        sl = slice(h * XA_HEAD_DIM, (h + 1) * XA_HEAD_DIM)
        s = _mm_nt(q[:, sl], k_ref[:, sl])
        p = jnp.exp(s - jnp.max(s, axis=-1, keepdims=True))
        p = p / jnp.sum(p, axis=-1, keepdims=True)
        outs.append(_mm(p.astype(BF16), v_ref[:, sl]))
    o = jnp.concatenate(outs, axis=-1).astype(BF16)
    o_ref[...] = x + _mm(o, wo_ref[...])


def _xattn(x, seq_len, g, wq, kv, wo):
    n, d = x.shape
    tm = _pick(seq_len, (512, 256))
    per_b = seq_len // tm
    m = MEM_TOKENS
    return pl.pallas_call(
        _xattn_kernel,
        grid=(n // tm,),
        in_specs=[pl.BlockSpec((tm, d), lambda i: (i, 0)),
                  pl.BlockSpec((1, d), lambda i: (0, 0)),
                  pl.BlockSpec((d, d), lambda i: (0, 0)),
                  pl.BlockSpec((m, d), lambda i: (i // per_b, 0)),
                  pl.BlockSpec((m, d), lambda i: (i // per_b, 1)),
                  pl.BlockSpec((d, d), lambda i: (0, 0))],
        out_specs=pl.BlockSpec((tm, d), lambda i: (i, 0)),
        out_shape=jax.ShapeDtypeStruct((n, d), F32),
        compiler_params=_cp("parallel"),
        name="xattn",
    )(x, g.reshape(1, d), wq, kv, kv, wo)


def _ffn_kernel(x_ref, g_ref, wg_ref, wu_ref, wd_ref, o_ref, xn_ref, acc_ref):
    f = pl.program_id(1)

    @pl.when(f == 0)
    def _():
        xn_ref[...] = _rms(x_ref[...], g_ref[...]).astype(BF16)
        acc_ref[...] = jnp.zeros_like(acc_ref)

    xn = xn_ref[...]
    hg = _mm(xn, wg_ref[...])
    hu = _mm(xn, wu_ref[...])
    a = (hg * _sigmoid(hg) * hu).astype(BF16)
    acc_ref[...] += _mm(a, wd_ref[...])

    @pl.when(f == pl.num_programs(1) - 1)
    def _():
        o_ref[...] = x_ref[...] + acc_ref[...]


def _ffn(x, g, w_gu, w_d):
    n, d = x.shape
    ff = w_d.shape[0]
    tm = _pick(n, (1024, 512, 256))
    tf = _pick(ff, (512, 256, 128))
    nf = ff // tf
    return pl.pallas_call(
        _ffn_kernel,
        grid=(n // tm, nf),
        in_specs=[pl.BlockSpec((tm, d), lambda i, f: (i, 0)),
                  pl.BlockSpec((1, d), lambda i, f: (0, 0)),
                  pl.BlockSpec((d, tf), lambda i, f: (0, f)),
                  pl.BlockSpec((d, tf), lambda i, f: (0, f + nf)),
                  pl.BlockSpec((tf, d), lambda i, f: (f, 0))],
        out_specs=pl.BlockSpec((tm, d), lambda i, f: (i, 0)),
        out_shape=jax.ShapeDtypeStruct((n, d), F32),
        scratch_shapes=[pltpu.VMEM((tm, d), BF16), pltpu.VMEM((tm, d), F32)],
        compiler_params=_cp("parallel", "arbitrary"),
        name="ffn",
    )(x, g.reshape(1, d), w_gu, w_gu, w_d)


def _router_kernel(x_ref, g_ref, wh_ref, wl_ref, comb_ref):
    xn = _rms(x_ref[...], g_ref[...])
    xh = xn.astype(BF16)
    xl = (xn - xh.astype(F32)).astype(BF16)
    logits = _mm(xh, wh_ref[...]) + _mm(xh, wl_ref[...]) + _mm(xl, wh_ref[...])
    lane = lax.broadcasted_iota(jnp.int32, logits.shape, 1)
    logits = jnp.where(lane < N_EXPERTS, logits, NEG_BIG)
    m1 = jnp.max(logits, axis=-1, keepdims=True)
    i1 = jnp.min(jnp.where(logits == m1, lane, 128), axis=-1, keepdims=True)
    rest = jnp.where(lane == i1, NEG_BIG, logits)
    m2 = jnp.max(rest, axis=-1, keepdims=True)
    i2 = jnp.min(jnp.where(rest == m2, lane, 128), axis=-1, keepdims=True)
    e2 = jnp.exp(m2 - m1)
    den = 1.0 + e2
    comb_ref[...] = jnp.where(lane == i1, 1.0 / den, 0.0) + jnp.where(lane == i2, e2 / den, 0.0)


def _router(x, g, w_router):
    n, d = x.shape
    tm = _pick(n, (1024, 512, 256))
    wp = jnp.zeros((d, 128), F32).at[:, :N_EXPERTS].set(w_router)
    wh = wp.astype(BF16)
    wl = (wp - wh.astype(F32)).astype(BF16)
    return pl.pallas_call(
        _router_kernel,
        grid=(n // tm,),
        in_specs=[pl.BlockSpec((tm, d), lambda i: (i, 0)),
                  pl.BlockSpec((1, d), lambda i: (0, 0)),
                  pl.BlockSpec((d, 128), lambda i: (0, 0)),
                  pl.BlockSpec((d, 128), lambda i: (0, 0))],
        out_specs=pl.BlockSpec((tm, 128), lambda i: (i, 0)),
        out_shape=jax.ShapeDtypeStruct((n, 128), F32),
        compiler_params=_cp("parallel"),
        name="router",
    )(x, g.reshape(1, d), wh, wl)


def _moe_kernel(x_ref, g_ref, comb_ref, wg_ref, wu_ref, wd_ref, o_ref, xn_ref, acc_ref):
    e = pl.program_id(1)
    f = pl.program_id(2)

    @pl.when((e == 0) & (f == 0))
    def _():
        xn_ref[...] = _rms(x_ref[...], g_ref[...]).astype(BF16)
        acc_ref[...] = jnp.zeros_like(acc_ref)

    comb = comb_ref[...]
    lane = lax.broadcasted_iota(jnp.int32, comb.shape, 1)
    ce = jnp.sum(jnp.where(lane == e, comb, 0.0), axis=-1, keepdims=True)
    xn = xn_ref[...]
    hg = _mm(xn, wg_ref[0])
    hu = _mm(xn, wu_ref[0])
    a = (hg * _sigmoid(hg) * hu * ce).astype(BF16)
    acc_ref[...] += _mm(a, wd_ref[0])

    @pl.when((e == pl.num_programs(1) - 1) & (f == pl.num_programs(2) - 1))
    def _():
        o_ref[...] = x_ref[...] + acc_ref[...]


def _moe(x, g, w_router, w_gu, w_d):
    n, d = x.shape
    ne, ff = w_d.shape[0], w_d.shape[1]
    comb = _router(x, g, w_router)
    tm = _pick(n, (1024, 512, 256))
    tf = _pick(ff, (512, 256, 128))
    nf = ff // tf
    return pl.pallas_call(
        _moe_kernel,
        grid=(n // tm, ne, nf),
        in_specs=[pl.BlockSpec((tm, d), lambda i, e, f: (i, 0)),
                  pl.BlockSpec((1, d), lambda i, e, f: (0, 0)),
                  pl.BlockSpec((tm, 128), lambda i, e, f: (i, 0)),
                  pl.BlockSpec((1, d, tf), lambda i, e, f: (e, 0, f)),
                  pl.BlockSpec((1, d, tf), lambda i, e, f: (e, 0, f + nf)),
                  pl.BlockSpec((1, tf, d), lambda i, e, f: (e, f, 0))],
        out_specs=pl.BlockSpec((tm, d), lambda i, e, f: (i, 0)),
        out_shape=jax.ShapeDtypeStruct((n, d), F32),
        scratch_shapes=[pltpu.VMEM((tm, d), BF16), pltpu.VMEM((tm, d), F32)],
        compiler_params=_cp("parallel", "arbitrary", "arbitrary"),
        name="moe",
    )(x, g.reshape(1, d), comb, w_gu, w_gu, w_d)


NA_GROUP = 4
NA_KEYS = NA_WIN_ROWS * GRID_W
NA_BLK = 8 * GRID_W


def _na_bias_table(rpb):
    c = jnp.arange(GRID_W)
    cs = jnp.clip(c - NA_WIN_COLS // 2, 0, GRID_W - NA_WIN_COLS)
    cp = jnp.arange(GRID_W)
    valid = (cp[None, :] >= cs[:, None]) & (cp[None, :] < cs[:, None] + NA_WIN_COLS)
    colrel = jnp.clip(cp[None, :] - c[:, None] + NA_WIN_COLS - 1, 0, 2 * NA_WIN_COLS - 2)
    rr = jnp.arange(NA_WIN_ROWS)[:, None] + jnp.arange(NA_WIN_ROWS)[None, :]
    tbl = rpb.astype(F32)[:, rr][:, :, :, colrel]
    tbl = jnp.where(valid[None, None, None], tbl, NEG_BIG)
    tbl = tbl.reshape(NA_HEADS // NA_GROUP, NA_GROUP, NA_WIN_ROWS, NA_WIN_ROWS, GRID_W, GRID_W)
    tbl = tbl.transpose(2, 0, 3, 5, 1, 4)
    return tbl.reshape(NA_WIN_ROWS, NA_HEADS // NA_GROUP, NA_KEYS, NA_GROUP * GRID_W)


def _na_kernel(q_ref, kp_ref, kc_ref, kn_ref, vp_ref, vc_ref, vn_ref, bias_ref, o_ref, kst, vst, *, rows):
    r = pl.program_id(0) % rows
    i = r // 8

    @pl.when(r % 8 == 0)
    def _():
        kst[0:NA_BLK] = kp_ref[...]
        kst[NA_BLK:2 * NA_BLK] = kc_ref[...]
        kst[2 * NA_BLK:3 * NA_BLK] = kn_ref[...]
        vst[0:NA_BLK] = vp_ref[...]
        vst[NA_BLK:2 * NA_BLK] = vc_ref[...]
        vst[2 * NA_BLK:3 * NA_BLK] = vn_ref[...]

    rs = jnp.clip(r - NA_WIN_ROWS // 2, 0, rows - NA_WIN_ROWS)
    off = pl.multiple_of((rs - 8 * (i - 1)) * GRID_W, GRID_W)
    gw = NA_GROUP * NA_HEAD_DIM
    lane_head = lax.broadcasted_iota(jnp.int32, (GRID_W, gw), 1) // NA_HEAD_DIM
    for grp in range(NA_HEADS // NA_GROUP):
        sl = slice(grp * gw, (grp + 1) * gw)
        qg = q_ref[:, sl]
        bq = jnp.concatenate([jnp.where(lane_head == h, qg, jnp.zeros_like(qg)) for h in range(NA_GROUP)], axis=0)
        kw = kst[pl.ds(off, NA_KEYS), sl]
        st = _mm_nt(kw, bq) * (NA_HEAD_DIM ** -0.5) + bias_ref[0, grp]
        p = jnp.exp(st - jnp.max(st, axis=0, keepdims=True))
        p = (p / jnp.sum(p, axis=0, keepdims=True)).astype(BF16)
        vw = vst[pl.ds(off, NA_KEYS), sl]
        res = _mm_tn(p, vw)
        o = jnp.zeros((GRID_W, gw), F32)
        for h in range(NA_GROUP):
            o = o + jnp.where(lane_head == h, res[h * GRID_W:(h + 1) * GRID_W], 0.0)
        o_ref[:, sl] = o.astype(o_ref.dtype)


def _na_attention(qkv, batch, seq_len, bias_tbl):
    n = qkv.shape[0]
    d = D_MODEL
    rows = seq_len // GRID_W
    assert rows % 8 == 0 and rows >= 16
    nb = rows // 8

    def kmap(delta, col):
        def f(g):
            b = g // rows
            i = (g % rows) // 8
            return (b * nb + jnp.clip(i + delta, 0, nb - 1), col)
        return f

    def bias_map(g):
        r = g % rows
        return (jnp.clip(r - NA_WIN_ROWS // 2, 0, rows - NA_WIN_ROWS) - r + NA_WIN_ROWS - 1, 0, 0, 0)

    kv_specs = [pl.BlockSpec((NA_BLK, d), kmap(dl, col)) for col in (1, 2) for dl in (-1, 0, 1)]
    return pl.pallas_call(
        functools.partial(_na_kernel, rows=rows),
        grid=(batch * rows,),
        in_specs=[pl.BlockSpec((GRID_W, d), lambda g: (g, 0))] + kv_specs
                 + [pl.BlockSpec((1,) + bias_tbl.shape[1:], bias_map)],
        out_specs=pl.BlockSpec((GRID_W, d), lambda g: (g, 0)),
        out_shape=jax.ShapeDtypeStruct((n, d), BF16),
        scratch_shapes=[pltpu.VMEM((3 * NA_BLK, d), BF16), pltpu.VMEM((3 * NA_BLK, d), BF16)],
        compiler_params=_cp("arbitrary"),
        name="na_attention",
    )(qkv, qkv, qkv, qkv, qkv, qkv, qkv, bias_tbl)


def _na_mixer(x, batch, seq_len, g, w_qkv, rpb, w_out):
    qkv = _norm_matmul(x, g, w_qkv)
    o = _na_attention(qkv, batch, seq_len, _na_bias_table(rpb))
    return _proj_res(lambda a: a, [(o, D_MODEL, 0, None)], [], w_out, x, "na_out")


LRU_LANES = 128


def _halo_maps(seq_len, tile, halo, col, tile_of):
    per_tile = tile // halo
    per_seq = seq_len // halo

    def prev(*g):
        return (jnp.maximum(g[0] * per_seq + tile_of(*g) * per_tile - 1, g[0] * per_seq), col)

    def nxt(*g):
        return (jnp.minimum(g[0] * per_seq + (tile_of(*g) + 1) * per_tile, (g[0] + 1) * per_seq - 1), col)

    return prev, nxt


def _lru_kernel(cur_ref, prev_ref, next_ref, cw_ref, cb_ref, wg_ref, ba_ref, bx_ref, lam_ref, o_ref,
                a_scr, b_scr, h_scr, carry, *, tile, n_tiles):
    d = pl.program_id(1)
    i = pl.program_id(2)
    ti = jnp.where(d == 0, i, n_tiles - 1 - i)
    hl = BF16_SUBLANES
    prev = jnp.where(ti == 0, 0.0, prev_ref[...].astype(F32))
    nxt = jnp.where(ti == n_tiles - 1, 0.0, next_ref[...].astype(F32))
    xp = jnp.concatenate([prev, cur_ref[...].astype(F32), nxt], axis=0)
    left = LRU_CONV // 2
    xc = cb_ref[...]
    for k in range(LRU_CONV):
        s = hl + k - left
        xc = xc + xp[s:s + tile] * cw_ref[k:k + 1, :]
    xcb = xc.astype(BF16)
    ga, gx = [], []
    for grp in range(D_MODEL // LRU_LANES):
        gg = _mm(xcb[:, grp * LRU_LANES:(grp + 1) * LRU_LANES], wg_ref[0, grp])
        ga.append(gg[:, :LRU_LANES])
        gx.append(gg[:, LRU_LANES:])
    ga = jnp.concatenate(ga, axis=-1) + ba_ref[0]
    gx = jnp.concatenate(gx, axis=-1) + bx_ref[0]
    log_a = -LRU_C * _sigmoid(ga) * _softplus(-lam_ref[0])
    a = jnp.exp(log_a)
    a_scr[...] = a
    b_scr[...] = jnp.sqrt(1.0 - a * a) * (_sigmoid(gx) * xc)

    @pl.when(i == 0)
    def _():
        carry[...] = jnp.zeros_like(carry)

    def body(t, h):
        tt = jnp.where(d == 0, t, tile - 1 - t)
        h = a_scr[pl.ds(tt, 1), :] * h + b_scr[pl.ds(tt, 1), :]
        h_scr[pl.ds(tt, 1), :] = h
        return h

    carry[...] = lax.fori_loop(0, tile, body, carry[...], unroll=8)
    o_ref[0] = h_scr[...].astype(o_ref.dtype)


def _lru_gate_weights(ga_w, gx_w):
    def bd(w):
        w = w.reshape(2, D_MODEL // LRU_LANES, 2, 64, 64)
        z = jnp.zeros_like(w[:, :, 0])
        top = jnp.concatenate([w[:, :, 0], z], axis=-1)
        bot = jnp.concatenate([z, w[:, :, 1]], axis=-1)
        return jnp.concatenate([top, bot], axis=-2)
    return jnp.concatenate([bd(ga_w), bd(gx_w)], axis=-1).astype(BF16)


def _lru_scan(gb, batch, seq_len, conv_w, conv_b, wg, ba, bx, lam):
    n = gb.shape[0]
    w = D_MODEL
    tile = _pick(seq_len, (512, 256))
    n_tiles = seq_len // tile
    hl = BF16_SUBLANES

    def tile_of(b, d, i):
        return jnp.where(d == 0, i, n_tiles - 1 - i)

    prev_map, next_map = _halo_maps(seq_len, tile, hl, 1, tile_of)
    row = lambda b, d, i: (d, 0, 0)
    return pl.pallas_call(
        functools.partial(_lru_kernel, tile=tile, n_tiles=n_tiles),
        grid=(batch, 2, n_tiles),
        in_specs=[pl.BlockSpec((tile, w), lambda b, d, i: (b * n_tiles + tile_of(b, d, i), 1)),
                  pl.BlockSpec((hl, w), prev_map),
                  pl.BlockSpec((hl, w), next_map),
                  pl.BlockSpec((LRU_CONV, w), lambda b, d, i: (0, 0)),
                  pl.BlockSpec((1, w), lambda b, d, i: (0, 0)),
                  pl.BlockSpec((1,) + wg.shape[1:], lambda b, d, i: (d, 0, 0, 0)),
                  pl.BlockSpec((1, 1, w), row), pl.BlockSpec((1, 1, w), row), pl.BlockSpec((1, 1, w), row)],
        out_specs=pl.BlockSpec((1, tile, w), lambda b, d, i: (d, b * n_tiles + tile_of(b, d, i), 0)),
        out_shape=jax.ShapeDtypeStruct((2, n, w), BF16),
        scratch_shapes=[pltpu.VMEM((tile, w), F32), pltpu.VMEM((tile, w), F32), pltpu.VMEM((tile, w), F32),
                        pltpu.VMEM((1, w), F32)],
        compiler_params=_cp("parallel", "arbitrary", "arbitrary"),
        name="lru_scan",
    )(gb, gb, gb, conv_w, conv_b.reshape(1, w), wg, ba.reshape(2, 1, w), bx.reshape(2, 1, w), lam.reshape(2, 1, w))


def _gelu_tanh(x):
    return 0.5 * x * (1.0 + jnp.tanh(math.sqrt(2.0 / math.pi) * (x + 0.044715 * (x * x * x))))


def _lru_mixer(x, batch, seq_len, g, w_in, conv_w, conv_b, ga_w, ga_b, gx_w, gx_b, lam, w_out):
    gb = _norm_matmul(x, g, w_in)
    h = _lru_scan(gb, batch, seq_len, conv_w, conv_b, _lru_gate_weights(ga_w, gx_w), ga_b, gx_b, lam)

    def prologue(gate, hf, hb):
        return _gelu_tanh(gate.astype(F32)) * (hf.astype(F32) + hb.astype(F32))

    return _proj_res(prologue, [(gb, D_MODEL, 0, None), (h, D_MODEL, 0, 0), (h, D_MODEL, 0, 1)], [], w_out, x,
                     "lru_out")


GLA_EXTRA = 128


def _gla_kernel(q_ref, k_ref, v_ref, lr_ref, w2_ref, gb_ref, o_ref, qb_scr, kb_scr, kd_scr, dec_scr, st_scr,
                *, tile, n_tiles):
    d = pl.program_id(1)
    i = pl.program_id(2)
    c = GLA_CHUNK
    n_chunks = tile // c

    @pl.when(i == 0)
    def _():
        st_scr[...] = jnp.zeros_like(st_scr)

    z = _mm(lr_ref[...], w2_ref[0]) + gb_ref[0]
    g = -_softplus(-z) / GLA_TAU
    row = lax.broadcasted_iota(jnp.int32, (tile, tile), 0)
    col = lax.broadcasted_iota(jnp.int32, (tile, tile), 1)
    same = (row // c) == (col // c)
    fwd = d == 0
    tri = same & (jnp.where(fwd, col, row) <= jnp.where(fwd, row, col))
    tri = jnp.where(tri, 1.0, 0.0).astype(BF16)
    ones = jnp.where(same, 1.0, 0.0).astype(BF16)
    g_hi = g.astype(BF16)
    g_lo = (g - g_hi.astype(F32)).astype(BF16)
    bsum = _mm(tri, g_hi) + _mm(tri, g_lo)
    tot = _mm(ones, g_hi) + _mm(ones, g_lo)
    q = q_ref[...].astype(F32) * (GLA_HK ** -0.5)
    k = k_ref[...].astype(F32)
    qb_scr[...] = (q * jnp.exp(bsum)).astype(BF16)
    kb_scr[...] = (k * jnp.exp(-bsum)).astype(BF16)
    kd_scr[...] = (k * jnp.exp(tot - bsum)).astype(BF16)
    dec_scr[...] = jnp.exp(tot)

    r64 = lax.broadcasted_iota(jnp.int32, (c, c), 0)
    c64 = lax.broadcasted_iota(jnp.int32, (c, c), 1)
    mask = jnp.where(fwd, c64, r64 + 1) <= jnp.where(fwd, r64, c64)
    for cc in range(n_chunks):
        ci = jnp.where(fwd, cc, n_chunks - 1 - cc)
        off = pl.multiple_of(ci * c, c)
        for h in range(GLA_HEADS):
            ks = slice(h * GLA_HK, (h + 1) * GLA_HK)
            vs = slice(h * GLA_HV, (h + 1) * GLA_HV)
            qb = qb_scr[pl.ds(off, c), ks]
            kb = kb_scr[pl.ds(off, c), ks]
            kd = kd_scr[pl.ds(off, c), ks]
            vv = v_ref[pl.ds(off, c), vs]
            att = jnp.where(mask, _mm_nt(qb, kb), 0.0).astype(BF16)
            s_t = st_scr[h]
            o = _mm(att, vv) + _mm_nt(qb, s_t.astype(BF16))
            o_ref[0, pl.ds(off, c), vs] = o.astype(o_ref.dtype)
            dec = dec_scr[pl.ds(off, 1), ks]
            st_scr[h] = s_t * dec + _mm_tn(vv, kd)


def _gla_scan(proj, batch, seq_len, w2p, gate_b):
    n = proj.shape[0]
    tile = _pick(seq_len, (256,))
    n_tiles = seq_len // tile

    def tok(b, d, i):
        return b * n_tiles + jnp.where(d == 0, i, n_tiles - 1 - i)

    return pl.pallas_call(
        functools.partial(_gla_kernel, tile=tile, n_tiles=n_tiles),
        grid=(batch, 2, n_tiles),
        in_specs=[pl.BlockSpec((tile, GLA_DK), lambda b, d, i: (tok(b, d, i), 0)),
                  pl.BlockSpec((tile, GLA_DK), lambda b, d, i: (tok(b, d, i), 1)),
                  pl.BlockSpec((tile, GLA_DV), lambda b, d, i: (tok(b, d, i), 1)),
                  pl.BlockSpec((tile, GLA_EXTRA), lambda b, d, i: (tok(b, d, i), (2 * GLA_DK + 2 * GLA_DV) // GLA_EXTRA)),
                  pl.BlockSpec((1, GLA_EXTRA, GLA_DK), lambda b, d, i: (d, 0, 0)),
                  pl.BlockSpec((1, 1, GLA_DK), lambda b, d, i: (d, 0, 0))],
        out_specs=pl.BlockSpec((1, tile, GLA_DV), lambda b, d, i: (d, tok(b, d, i), 0)),
        out_shape=jax.ShapeDtypeStruct((2, n, GLA_DV), BF16),
        scratch_shapes=[pltpu.VMEM((tile, GLA_DK), BF16), pltpu.VMEM((tile, GLA_DK), BF16),
                        pltpu.VMEM((tile, GLA_DK), BF16), pltpu.VMEM((tile, GLA_DK), F32),
                        pltpu.VMEM((GLA_HEADS, GLA_HV, GLA_HK), F32)],
        compiler_params=_cp("parallel", "arbitrary", "arbitrary"),
        name="gla_scan",
    )(proj, proj, proj, proj, w2p, gate_b.reshape(2, 1, GLA_DK))


def _gla_mixer(x, batch, seq_len, g, w_in, gate_w1, gate_w2, gate_b, head_norm, w_out):
    d = D_MODEL
    pad = jnp.zeros((d, GLA_EXTRA - 2 * GLA_GATE_RANK), w_in.dtype)
    w_ext = jnp.concatenate([w_in, gate_w1[0].astype(w_in.dtype), gate_w1[1].astype(w_in.dtype), pad], axis=1)
    proj = _norm_matmul(x, g, w_ext)
    w2p = jnp.zeros((2, GLA_EXTRA, GLA_DK), F32)
    w2p = w2p.at[0, :GLA_GATE_RANK].set(gate_w2[0]).at[1, GLA_GATE_RANK:2 * GLA_GATE_RANK].set(gate_w2[1])
    o = _gla_scan(proj, batch, seq_len, w2p.astype(BF16), gate_b)

    def prologue(of, ob, r, hn):
        o_sum = of.astype(F32) + ob.astype(F32)
        parts = []
        for h in range(GLA_HEADS):
            oh = o_sum[:, h * GLA_HV:(h + 1) * GLA_HV]
            parts.append(_rms(oh, hn))
        r = r.astype(F32)
        return jnp.concatenate(parts, axis=-1) * (r * _sigmoid(r))

    return _proj_res(prologue, [(o, GLA_DV, 0, 0), (o, GLA_DV, 0, 1), (proj, GLA_DV, 2, None)],
                     [head_norm.reshape(1, GLA_HV).astype(F32)], w_out, x, "gla_out")


HY_PAD = 128


def _hy_filter_kernel(bands_ref, w1t_ref, w1c_ref, w1s_ref, b1_ref, w2_ref, b2_ref, w3_ref, freq_ref, ld_ref,
                      h_ref, s_ref, *, seq_len, tile):
    i = pl.program_id(0)
    hp = lax.Precision.HIGHEST
    j = (i * tile + lax.broadcasted_iota(jnp.int32, (tile, 1), 0)).astype(F32)
    t = j / seq_len
    ang = (2.0 * math.pi / seq_len) * j * bands_ref[...]
    pre = (t * w1t_ref[...]
           + jnp.dot(jnp.cos(ang), w1c_ref[...], precision=hp, preferred_element_type=F32)
           + jnp.dot(jnp.sin(-ang), w1s_ref[...], precision=hp, preferred_element_type=F32)
           + b1_ref[...])
    h = jnp.sin(freq_ref[0:1, :] * pre)
    h = jnp.sin(freq_ref[1:2, :] * (jnp.dot(h, w2_ref[...], precision=hp, preferred_element_type=F32) + b2_ref[...]))
    h = jnp.dot(h, w3_ref[...], precision=hp, preferred_element_type=F32)
    dist = jnp.abs(j - (seq_len // 2)) / (seq_len / 2)
    h = h * jnp.exp(-dist * jnp.exp(ld_ref[...]))
    h_ref[...] = h.astype(h_ref.dtype)

    @pl.when(i == 0)
    def _():
        s_ref[...] = jnp.zeros_like(s_ref)

    s_ref[...] += jnp.sum(jnp.abs(h), axis=0, keepdims=True)


def _hy_filter(seq_len, w1, b1, w2, b2, w3, freq, log_decay):
    hid = HY_FILTER_HID
    c2 = w3.shape[1]
    tile = _pick(seq_len, (512, 256))
    bands = jnp.zeros((1, HY_PAD), F32).at[0, :HY_BANDS].set(jnp.linspace(1e-4, HY_BANDS - 1, HY_BANDS, dtype=F32))
    w1 = w1.astype(F32)
    w1c = jnp.zeros((HY_PAD, hid), F32).at[:HY_BANDS].set(w1[1:1 + HY_BANDS])
    w1s = jnp.zeros((HY_PAD, hid), F32).at[:HY_BANDS].set(w1[1 + HY_BANDS:])
    full = lambda shape: pl.BlockSpec(shape, lambda i: (0, 0))
    return pl.pallas_call(
        functools.partial(_hy_filter_kernel, seq_len=seq_len, tile=tile),
        grid=(seq_len // tile,),
        in_specs=[full((1, HY_PAD)), full((1, hid)), full((HY_PAD, hid)), full((HY_PAD, hid)), full((1, hid)),
                  full((hid, hid)), full((1, hid)), full((hid, c2)), full((2, hid)), full((1, c2))],
        out_specs=[pl.BlockSpec((tile, c2), lambda i: (i, 0)), pl.BlockSpec((1, c2), lambda i: (0, 0))],
        out_shape=[jax.ShapeDtypeStruct((seq_len, c2), BF16), jax.ShapeDtypeStruct((1, c2), F32)],
        compiler_params=_cp("arbitrary"),
        name="hy_filter",
    )(bands, w1[0:1], w1c, w1s, b1.reshape(1, hid).astype(F32), w2.astype(F32), b2.reshape(1, hid).astype(F32),
      w3.astype(F32), freq.astype(F32), log_decay.reshape(1, c2).astype(F32))


def _hy_pre_kernel(cur_ref, prev_ref, next_ref, w_ref, b_ref, o_ref, *, tile, n_tiles):
    i = pl.program_id(1)
    hl = BF16_SUBLANES
    prev = jnp.where(i == 0, 0.0, prev_ref[...].astype(F32))
    nxt = jnp.where(i == n_tiles - 1, 0.0, next_ref[...].astype(F32))
    xp = jnp.concatenate([prev, cur_ref[...].astype(F32), nxt], axis=0)
    y = b_ref[...]
    for k in range(3):
        s = hl + k - 1
        y = y + xp[s:s + tile] * w_ref[k:k + 1, :]
    o_ref[...] = y.astype(o_ref.dtype)


def _hy_pre(u, batch, seq_len, short_w, short_b):
    n, c3 = u.shape
    w = D_MODEL
    tile = _pick(seq_len, (512, 256))
    n_tiles = seq_len // tile
    hl = BF16_SUBLANES
    per_tile, per_seq = tile // hl, seq_len // hl
    return pl.pallas_call(
        functools.partial(_hy_pre_kernel, tile=tile, n_tiles=n_tiles),
        grid=(batch, n_tiles, c3 // w),
        in_specs=[pl.BlockSpec((tile, w), lambda b, i, c: (b * n_tiles + i, c)),
                  pl.BlockSpec((hl, w), lambda b, i, c: (jnp.maximum(b * per_seq + i * per_tile - 1, b * per_seq), c)),
                  pl.BlockSpec((hl, w), lambda b, i, c: (jnp.minimum(b * per_seq + (i + 1) * per_tile,
                                                                     (b + 1) * per_seq - 1), c)),
                  pl.BlockSpec((3, w), lambda b, i, c: (0, c)),
                  pl.BlockSpec((1, w), lambda b, i, c: (0, c))],
        out_specs=pl.BlockSpec((tile, w), lambda b, i, c: (b * n_tiles + i, c)),
        out_shape=jax.ShapeDtypeStruct((n, c3), BF16),
        compiler_params=_cp("parallel", "parallel", "parallel"),
        name="hy_pre",
    )(u, u, u, short_w.astype(F32), short_b.reshape(1, c3).astype(F32))


def _fft_plan(seq_len):
    n = 2 * seq_len
    n2 = 256 if n >= 32768 else 128
    n1 = n // n2
    assert n1 % 4 == 0 and n1 * n2 == n
    return n, n1, n2, n1 // 2


def _fft_tables(seq_len):
    n, n1, n2, ah = _fft_plan(seq_len)
    b = jnp.arange(n2, dtype=jnp.int32)
    ka = jnp.arange(n1, dtype=jnp.int32)
    a = jnp.arange(ah, dtype=jnp.int32)
    w = 2.0 * math.pi / n
    m = (ka[None, :, None] * (b[:, None, None] + n2 * a[None, None, :])) % n
    ang = w * m.astype(F32)
    ft = jnp.concatenate([jnp.cos(ang), -jnp.sin(ang)], axis=1).astype(BF16)
    ap = a + n1 // 4
    m = ((b[:, None, None] + n2 * ap[None, :, None]) * ka[None, None, :]) % n
    ang = w * m.astype(F32)
    fti = (jnp.concatenate([jnp.cos(ang), -jnp.sin(ang)], axis=2) / n).astype(BF16)
    ang = (2.0 * math.pi / n2) * ((b[:, None] * b[None, :]) % n2).astype(F32)
    fr, fi = jnp.cos(ang), -jnp.sin(ang)
    m2 = jnp.block([[fr, -fi], [fi, fr]]).astype(BF16)
    m2i = jnp.block([[fr, fi], [-fi, fr]]).astype(BF16)
    return ft, fti, m2, m2i


def _fft_first_kernel(z_ref, ft_ref, o_ref, *, tb):
    for i in range(tb):
        o_ref[0, i] = _mm(ft_ref[i], z_ref[0, i]).astype(o_ref.dtype)


def _fft_first(zb, ft):
    bsz, n2, ah, c = zb.shape
    rows = ft.shape[1]
    cb = 1024
    tb = _pick(n2, (16, 8))
    return pl.pallas_call(
        functools.partial(_fft_first_kernel, tb=tb),
        grid=(bsz, n2 // tb, c // cb),
        in_specs=[pl.BlockSpec((1, tb, ah, cb), lambda b, j, cc: (b, j, 0, cc)),
                  pl.BlockSpec((tb, rows, ah), lambda b, j, cc: (j, 0, 0))],
        out_specs=pl.BlockSpec((1, tb, rows, cb), lambda b, j, cc: (b, j, 0, cc)),
        out_shape=jax.ShapeDtypeStruct((bsz, n2, rows, c), BF16),
        compiler_params=_cp("parallel", "parallel", "parallel"),
        name="fft_first",
    )(zb, ft)


def _fft_spectrum_kernel(a_ref, s_ref, m_ref, o_ref, *, tk):
    for i in range(tk):
        o_ref[0, i] = (_mm(m_ref[...], a_ref[0, i]) / s_ref[...]).astype(o_ref.dtype)


def _fft_conv_kernel(a_ref, h_ref, m_ref, mi_ref, o_ref, *, tk, n2):
    for i in range(tk):
        x = _mm(m_ref[...], a_ref[0, i])
        xr, xi = x[:n2], x[n2:]
        hr, hi = h_ref[0, i, :n2].astype(F32), h_ref[0, i, n2:].astype(F32)
        p = jnp.concatenate([xr * hr - xi * hi, xr * hi + xi * hr], axis=0).astype(BF16)
        o_ref[0, i] = _mm(mi_ref[...], p).astype(o_ref.dtype)


def _fft_mid(ak, m2, m2i=None, spec=None, spec_col=0, scale=None):
    bsz, n1, r2, c = ak.shape
    cb = 1024
    tk = _pick(n1, (4, 2))
    grid = (bsz, n1 // tk, c // cb)
    a_spec = pl.BlockSpec((1, tk, r2, cb), lambda b, k, cc: (b, k, 0, cc))
    m_spec = pl.BlockSpec((r2, r2), lambda b, k, cc: (0, 0))
    if spec is None:
        return pl.pallas_call(
            functools.partial(_fft_spectrum_kernel, tk=tk),
            grid=grid,
            in_specs=[a_spec, pl.BlockSpec((1, cb), lambda b, k, cc: (0, cc)), m_spec],
            out_specs=a_spec,
            out_shape=jax.ShapeDtypeStruct(ak.shape, BF16),
            compiler_params=_cp("parallel", "parallel", "parallel"),
            name="fft_spectrum",
        )(ak, scale, m2)
    return pl.pallas_call(
        functools.partial(_fft_conv_kernel, tk=tk, n2=r2 // 2),
        grid=grid,
        in_specs=[a_spec, pl.BlockSpec((1, tk, r2, cb), lambda b, k, cc: (0, k, 0, spec_col + cc)), m_spec, m_spec],
        out_specs=a_spec,
        out_shape=jax.ShapeDtypeStruct(ak.shape, BF16),
        compiler_params=_cp("parallel", "parallel", "parallel"),
        name="fft_conv",
    )(ak, spec, m2, m2i)


def _fft_last_kernel(c_ref, fti_ref, o_ref, *, tb):
    for i in range(tb):
        o_ref[0, i] = _mm(fti_ref[i], c_ref[0, i]).astype(o_ref.dtype)


def _fft_last(cb_arr, fti):
    bsz, n2, rows, c = cb_arr.shape
    ah = fti.shape[1]
    cb = 1024
    tb = _pick(n2, (16, 8))
    return pl.pallas_call(
        functools.partial(_fft_last_kernel, tb=tb),
        grid=(bsz, n2 // tb, c // cb),
        in_specs=[pl.BlockSpec((1, tb, rows, cb), lambda b, j, cc: (b, j, 0, cc)),
                  pl.BlockSpec((tb, ah, rows), lambda b, j, cc: (j, 0, 0))],
        out_specs=pl.BlockSpec((1, tb, ah, cb), lambda b, j, cc: (b, j, 0, cc)),
        out_shape=jax.ShapeDtypeStruct((bsz, n2, ah, c), BF16),
        compiler_params=_cp("parallel", "parallel", "parallel"),
        name="fft_last",
    )(cb_arr, fti)


def _to_b_major(z, bsz, n2, ah):
    c = z.shape[-1]
    return z.reshape(bsz, ah, n2, c).transpose(0, 2, 1, 3)


def _swap_digits(arr, inner):
    bsz, p, r, c = arr.shape
    return arr.reshape(bsz, p, 2, inner, c).transpose(0, 3, 2, 1, 4).reshape(bsz, inner, 2 * p, c)


def _fft_forward(z, bsz, seq_len, ft):
    _, n1, n2, ah = _fft_plan(seq_len)
    a = _fft_first(_to_b_major(z, bsz, n2, ah), ft)
    return _swap_digits(a, n1)


def _long_conv(z, bsz, seq_len, tables, spec, order):
    ft, fti, m2, m2i = tables
    _, n1, n2, ah = _fft_plan(seq_len)
    ak = _fft_forward(z, bsz, seq_len, ft)
    ck = _fft_mid(ak, m2, m2i, spec=spec, spec_col=order * (D_MODEL // 1024))
    y = _fft_last(_swap_digits(ck, n2), fti)
    return y.transpose(0, 2, 1, 3).reshape(bsz * seq_len, z.shape[-1])


def _hy_gate_kernel(x1_ref, y_ref, v_ref, skip_ref, o_ref):
    v = v_ref[...].astype(F32)
    o_ref[...] = (x1_ref[...].astype(F32) * (y_ref[...].astype(F32) + skip_ref[...] * v)).astype(o_ref.dtype)


def _hy_gate(u3, y, skip0):
    n, w = y.shape
    tm = _pick(n, (1024, 512, 256))
    return pl.pallas_call(
        _hy_gate_kernel,
        grid=(n // tm,),
        in_specs=[pl.BlockSpec((tm, w), lambda i: (i, 1)), pl.BlockSpec((tm, w), lambda i: (i, 0)),
                  pl.BlockSpec((tm, w), lambda i: (i, 0)), pl.BlockSpec((1, w), lambda i: (0, 0))],
        out_specs=pl.BlockSpec((tm, w), lambda i: (i, 0)),
        out_shape=jax.ShapeDtypeStruct((n, w), BF16),
        compiler_params=_cp("parallel"),
        name="hy_gate",
    )(u3, y, u3, skip0.reshape(1, w).astype(F32))


def _hyena_mixer(x, batch, seq_len, g, w_in, short_w, short_b, fw1, fb1, fw2, fb2, fw3, ffreq, log_decay, skip,
                 w_out):
    d = D_MODEL
    tables = _fft_tables(seq_len)
    filt, l1 = _hy_filter(seq_len, fw1, fb1, fw2, fb2, fw3, ffreq, log_decay)
    fk = _fft_forward(filt, 1, seq_len, tables[0])
    spec = _fft_mid(fk, tables[2], scale=l1)
    u3 = _hy_pre(_norm_matmul(x, g, w_in), batch, seq_len, short_w, short_b)
    y1 = _long_conv(u3[:, :d], batch, seq_len, tables, spec, 0)
    z1 = _hy_gate(u3, y1, skip[0])
    y2 = _long_conv(z1, batch, seq_len, tables, spec, 1)

    def prologue(x2, yy, zz, sk):
        return x2.astype(F32) * (yy.astype(F32) + sk * zz.astype(F32))

    return _proj_res(prologue, [(u3, d, 2, None), (y2, d, 0, None), (z1, d, 0, None)],
                     [skip[1].reshape(1, d).astype(F32)], w_out, x, "hy_out")


def _trunk(x, mem, p):
    batch, seq_len, d = x.shape
    x = x.reshape(batch * seq_len, d)
    mem = mem.reshape(batch * MEM_TOKENS, d)
    depth = p["norm_mix"].shape[0]
    for i in range(depth):
        m, j = i % 4, i // 4
        g = p["norm_mix"][i]
        if m == 0:
            x = _na_mixer(x, batch, seq_len, g, p["na_w_qkv"][j], p["na_rpb"][j], p["na_w_out"][j])
        elif m == 1:
            x = _lru_mixer(x, batch, seq_len, g, p["lru_w_in"][j], p["lru_conv_w"][j], p["lru_conv_b"][j],
                           p["lru_gate_a_w"][j], p["lru_gate_a_b"][j], p["lru_gate_x_w"][j], p["lru_gate_x_b"][j],
                           p["lru_lambda"][j], p["lru_w_out"][j])
        elif m == 2:
            x = _gla_mixer(x, batch, seq_len, g, p["gla_w_in"][j], p["gla_gate_w1"][j], p["gla_gate_w2"][j],
                           p["gla_gate_b"][j], p["gla_head_norm"][j], p["gla_w_out"][j])
        else:
            x = _hyena_mixer(x, batch, seq_len, g, p["hy_w_in"][j], p["hy_short_w"][j], p["hy_short_b"][j],
                             p["hy_filt_w1"][j], p["hy_filt_b1"][j], p["hy_filt_w2"][j], p["hy_filt_b2"][j],
                             p["hy_filt_w3"][j], p["hy_filt_freq"][j], p["hy_log_decay"][j], p["hy_skip"][j],
                             p["hy_w_out"][j])
        kv = _norm_matmul(mem, p["norm_mem"][i], p["xa_w_kv"][i])
        x = _xattn(x, seq_len, p["norm_xattn"][i], p["xa_w_q"][i], kv, p["xa_w_out"][i])
        if i % 2 == 0:
            x = _ffn(x, p["norm_ffn"][i], p["ffn_w_gate_up"][i // 2], p["ffn_w_down"][i // 2])
        else:
            x = _moe(x, p["norm_ffn"][i], p["moe_router"][i // 2], p["moe_w_gate_up"][i // 2], p["moe_w_down"][i // 2])
    return _final_norm(x, p["norm_final"]).reshape(batch, seq_len, d)


_MATMUL_WEIGHTS = ("na_w_qkv", "na_w_out", "lru_w_in", "lru_w_out", "gla_w_in", "gla_w_out", "hy_w_in", "hy_w_out",
                   "xa_w_q", "xa_w_kv", "xa_w_out", "ffn_w_gate_up", "ffn_w_down", "moe_w_gate_up", "moe_w_down")


def kernel(x_prompt, x_sample, mem_prompt, mem_sample, norm_mix, norm_xattn, norm_mem, norm_ffn, norm_final, na_w_qkv, na_rpb, na_w_out, lru_w_in, lru_conv_w, lru_conv_b, lru_gate_a_w, lru_gate_a_b, lru_gate_x_w, lru_gate_x_b, lru_lambda, lru_w_out, gla_w_in, gla_gate_w1, gla_gate_w2, gla_gate_b, gla_head_norm, gla_w_out, hy_w_in, hy_short_w, hy_short_b, hy_filt_w1, hy_filt_b1, hy_filt_w2, hy_filt_b2, hy_filt_w3, hy_filt_freq, hy_log_decay, hy_skip, hy_w_out, xa_w_q, xa_w_kv, xa_w_out, ffn_w_gate_up, ffn_w_down, moe_router, moe_w_gate_up, moe_w_down):
    p = dict(norm_mix=norm_mix, norm_xattn=norm_xattn, norm_mem=norm_mem, norm_ffn=norm_ffn, norm_final=norm_final,
             na_w_qkv=na_w_qkv, na_rpb=na_rpb, na_w_out=na_w_out,
             lru_w_in=lru_w_in, lru_conv_w=lru_conv_w, lru_conv_b=lru_conv_b, lru_gate_a_w=lru_gate_a_w,
             lru_gate_a_b=lru_gate_a_b, lru_gate_x_w=lru_gate_x_w, lru_gate_x_b=lru_gate_x_b, lru_lambda=lru_lambda,
             lru_w_out=lru_w_out,
             gla_w_in=gla_w_in, gla_gate_w1=gla_gate_w1, gla_gate_w2=gla_gate_w2, gla_gate_b=gla_gate_b,
             gla_head_norm=gla_head_norm, gla_w_out=gla_w_out,
             hy_w_in=hy_w_in, hy_short_w=hy_short_w, hy_short_b=hy_short_b, hy_filt_w1=hy_filt_w1,
             hy_filt_b1=hy_filt_b1, hy_filt_w2=hy_filt_w2, hy_filt_b2=hy_filt_b2, hy_filt_w3=hy_filt_w3,
             hy_filt_freq=hy_filt_freq, hy_log_decay=hy_log_decay, hy_skip=hy_skip, hy_w_out=hy_w_out,
             xa_w_q=xa_w_q, xa_w_kv=xa_w_kv, xa_w_out=xa_w_out,
             ffn_w_gate_up=ffn_w_gate_up, ffn_w_down=ffn_w_down,
             moe_router=moe_router, moe_w_gate_up=moe_w_gate_up, moe_w_down=moe_w_down)
    for name in _MATMUL_WEIGHTS:
        p[name] = p[name].astype(BF16)
    return (_trunk(x_prompt, mem_prompt, p), _trunk(x_sample, mem_sample, p))
```

```python
import functools
import math

import jax
import jax.numpy as jnp
from jax import lax
from jax.experimental import pallas as pl
from jax.experimental.pallas import tpu as pltpu

F32 = jnp.float32
BF16 = jnp.bfloat16

D_MODEL = 1024
RMS_EPS = 1e-6
GRID_W = 64
NA_HEADS = 16
NA_HEAD_DIM = 64
NA_WIN_ROWS = 8
NA_WIN_COLS = 16
LRU_C = 8.0
LRU_CONV = 4
GLA_HEADS = 4
GLA_DK = 512
GLA_DV = 1024
GLA_HK = 128
GLA_HV = 256
GLA_GATE_RANK = 16
GLA_TAU = 16.0
GLA_CHUNK = 64
HY_BANDS = 16
HY_FILTER_HID = 64
MEM_TOKENS = 256
XA_HEADS = 4
XA_HEAD_DIM = 256
D_FF = 3584
N_EXPERTS = 8
NEG_BIG = -1e30

V7X_VMEM_BYTES = 64 * 1024 * 1024
VMEM_LIMIT = V7X_VMEM_BYTES - 8 * 1024 * 1024
BF16_SUBLANES = 16


def _cp(*sem):
    return pltpu.CompilerParams(dimension_semantics=sem, vmem_limit_bytes=VMEM_LIMIT)


def _pick(n, cands):
    for c in cands:
        if n % c == 0:
            return c
    return n


def _rms(x, g):
    return x * lax.rsqrt(jnp.mean(x * x, axis=-1, keepdims=True) + RMS_EPS) * g


def _sigmoid(x):
    return 1.0 / (1.0 + jnp.exp(-x))


def _softplus(x):
    return jnp.maximum(x, 0.0) + jnp.log1p(jnp.exp(-jnp.abs(x)))


def _mm(a, b):
    return jnp.dot(a, b, preferred_element_type=F32)


def _mm_nt(a, b):
    return lax.dot_general(a, b, (((1,), (1,)), ((), ())), preferred_element_type=F32)


def _mm_tn(a, b):
    return lax.dot_general(a, b, (((0,), (0,)), ((), ())), preferred_element_type=F32)


def _norm_matmul_kernel(x_ref, g_ref, w_ref, o_ref, xn_ref):
    @pl.when(pl.program_id(1) == 0)
    def _():
        xn_ref[...] = _rms(x_ref[...], g_ref[...]).astype(BF16)

    o_ref[...] = _mm(xn_ref[...], w_ref[...]).astype(o_ref.dtype)


def _norm_matmul(x, g, w, out_dtype=BF16):
    n, d = x.shape
    f = w.shape[1]
    tm = _pick(n, (1024, 512, 256))
    tn = _pick(f, (1024, 640, 512, 256, 128))
    return pl.pallas_call(
        _norm_matmul_kernel,
        grid=(n // tm, f // tn),
        in_specs=[pl.BlockSpec((tm, d), lambda i, j: (i, 0)),
                  pl.BlockSpec((1, d), lambda i, j: (0, 0)),
                  pl.BlockSpec((d, tn), lambda i, j: (0, j))],
        out_specs=pl.BlockSpec((tm, tn), lambda i, j: (i, j)),
        out_shape=jax.ShapeDtypeStruct((n, f), out_dtype),
        scratch_shapes=[pltpu.VMEM((tm, d), BF16)],
        compiler_params=_cp("parallel", "arbitrary"),
        name="norm_matmul",
    )(x, g.reshape(1, d), w)


def _final_norm_kernel(x_ref, g_ref, o_ref):
    o_ref[...] = _rms(x_ref[...], g_ref[...])


def _final_norm(x, g):
    n, d = x.shape
    tm = _pick(n, (1024, 512, 256))
    return pl.pallas_call(
        _final_norm_kernel,
        grid=(n // tm,),
        in_specs=[pl.BlockSpec((tm, d), lambda i: (i, 0)), pl.BlockSpec((1, d), lambda i: (0, 0))],
        out_specs=pl.BlockSpec((tm, d), lambda i: (i, 0)),
        out_shape=jax.ShapeDtypeStruct((n, d), F32),
        compiler_params=_cp("parallel"),
        name="final_norm",
    )(x, g.reshape(1, d))


def _proj_res(prologue, tok_inputs, row_inputs, w, res, name):
    n, d = res.shape
    k = w.shape[0]
    tm = _pick(n, (512, 256))
    n_tok, n_row = len(tok_inputs), len(row_inputs)

    def kern(*refs):
        toks = []
        for r, (_, _, _, lead) in zip(refs[:n_tok], tok_inputs):
            toks.append(r[0] if lead is not None else r[...])
        rows = [r[...] for r in refs[n_tok:n_tok + n_row]]
        w_ref, res_ref, o_ref = refs[n_tok + n_row:]
        a = prologue(*toks, *rows)
        o_ref[...] = res_ref[...] + _mm(a.astype(BF16), w_ref[...])

    in_specs, args = [], []
    for arr, width, col, lead in tok_inputs:
        if lead is None:
            in_specs.append(pl.BlockSpec((tm, width), lambda i, c=col: (i, c)))
        else:
            in_specs.append(pl.BlockSpec((1, tm, width), lambda i, c=col, l=lead: (l, i, c)))
        args.append(arr)
    for arr in row_inputs:
        in_specs.append(pl.BlockSpec(arr.shape, lambda i: (0, 0)))
        args.append(arr)
    in_specs += [pl.BlockSpec((k, d), lambda i: (0, 0)), pl.BlockSpec((tm, d), lambda i: (i, 0))]
    args += [w, res]
    return pl.pallas_call(
        kern,
        grid=(n // tm,),
        in_specs=in_specs,
        out_specs=pl.BlockSpec((tm, d), lambda i: (i, 0)),
        out_shape=jax.ShapeDtypeStruct((n, d), F32),
        compiler_params=_cp("parallel"),
        name=name,
    )(*args)


def _xattn_kernel(x_ref, g_ref, wq_ref, k_ref, v_ref, wo_ref, o_ref):
    x = x_ref[...]
    xn = _rms(x, g_ref[...]).astype(BF16)
    q = (_mm(xn, wq_ref[...]) * (XA_HEAD_DIM ** -0.5)).astype(BF16)
    outs = []
    for h in range(XA_HEADS):
        sl = slice(h * XA_HEAD_DIM, (h + 1) * XA_HEAD_DIM)
        s = _mm_nt(q[:, sl], k_ref[:, sl])
        p = jnp.exp(s - jnp.max(s, axis=-1, keepdims=True))
        p = p / jnp.sum(p, axis=-1, keepdims=True)
        outs.append(_mm(p.astype(BF16), v_ref[:, sl]))
    o = jnp.concatenate(outs, axis=-1).astype(BF16)
    o_ref[...] = x + _mm(o, wo_ref[...])


def _xattn(x, seq_len, g, wq, kv, wo):
    n, d = x.shape
    tm = _pick(seq_len, (512, 256))
    per_b = seq_len // tm
    m = MEM_TOKENS
    return pl.pallas_call(
        _xattn_kernel,
        grid=(n // tm,),
        in_specs=[pl.BlockSpec((tm, d), lambda i: (i, 0)),
                  pl.BlockSpec((1, d), lambda i: (0, 0)),
                  pl.BlockSpec((d, d), lambda i: (0, 0)),
                  pl.BlockSpec((m, d), lambda i: (i // per_b, 0)),
                  pl.BlockSpec((m, d), lambda i: (i // per_b, 1)),
                  pl.BlockSpec((d, d), lambda i: (0, 0))],
        out_specs=pl.BlockSpec((tm, d), lambda i: (i, 0)),
        out_shape=jax.ShapeDtypeStruct((n, d), F32),
        compiler_params=_cp("parallel"),
        name="xattn",
    )(x, g.reshape(1, d), wq, kv, kv, wo)


def _ffn_kernel(x_ref, g_ref, wg_ref, wu_ref, wd_ref, o_ref, xn_ref, acc_ref):
    f = pl.program_id(1)

    @pl.when(f == 0)
    def _():
        xn_ref[...] = _rms(x_ref[...], g_ref[...]).astype(BF16)
        acc_ref[...] = jnp.zeros_like(acc_ref)

    xn = xn_ref[...]
    hg = _mm(xn, wg_ref[...])
    hu = _mm(xn, wu_ref[...])
    a = (hg * _sigmoid(hg) * hu).astype(BF16)
    acc_ref[...] += _mm(a, wd_ref[...])

    @pl.when(f == pl.num_programs(1) - 1)
    def _():
        o_ref[...] = x_ref[...] + acc_ref[...]


def _ffn(x, g, w_gu, w_d):
    n, d = x.shape
    ff = w_d.shape[0]
    tm = _pick(n, (1024, 512, 256))
    tf = _pick(ff, (512, 256, 128))
    nf = ff // tf
    return pl.pallas_call(
        _ffn_kernel,
        grid=(n // tm, nf),
        in_specs=[pl.BlockSpec((tm, d), lambda i, f: (i, 0)),
                  pl.BlockSpec((1, d), lambda i, f: (0, 0)),
                  pl.BlockSpec((d, tf), lambda i, f: (0, f)),
                  pl.BlockSpec((d, tf), lambda i, f: (0, f + nf)),
                  pl.BlockSpec((tf, d), lambda i, f: (f, 0))],
        out_specs=pl.BlockSpec((tm, d), lambda i, f: (i, 0)),
        out_shape=jax.ShapeDtypeStruct((n, d), F32),
        scratch_shapes=[pltpu.VMEM((tm, d), BF16), pltpu.VMEM((tm, d), F32)],
        compiler_params=_cp("parallel", "arbitrary"),
        name="ffn",
    )(x, g.reshape(1, d), w_gu, w_gu, w_d)


def _router_kernel(x_ref, g_ref, wh_ref, wl_ref, idx_ref, wts_ref):
    xn = _rms(x_ref[...], g_ref[...])
    xh = xn.astype(BF16)
    xl = (xn - xh.astype(F32)).astype(BF16)
    logits = _mm(xh, wh_ref[...]) + _mm(xh, wl_ref[...]) + _mm(xl, wh_ref[...])
    lane = lax.broadcasted_iota(jnp.int32, logits.shape, 1)
    logits = jnp.where(lane < N_EXPERTS, logits, NEG_BIG)
    m1 = jnp.max(logits, axis=-1, keepdims=True)
    i1 = jnp.min(jnp.where(logits == m1, lane, 128), axis=-1, keepdims=True)
    rest = jnp.where(lane == i1, NEG_BIG, logits)
    m2 = jnp.max(rest, axis=-1, keepdims=True)
    i2 = jnp.min(jnp.where(rest == m2, lane, 128), axis=-1, keepdims=True)
    e2 = jnp.exp(m2 - m1)
    den = 1.0 + e2
    idx_ref[...] = jnp.where(lane == 0, i1, jnp.where(lane == 1, i2, 0))
    wts_ref[...] = jnp.where(lane == 0, 1.0 / den, jnp.where(lane == 1, e2 / den, 0.0))


def _router(x, g, w_router):
    n, d = x.shape
    tm = _pick(n, (1024, 512, 256))
    wp = jnp.zeros((d, 128), F32).at[:, :N_EXPERTS].set(w_router)
    wh = wp.astype(BF16)
    wl = (wp - wh.astype(F32)).astype(BF16)
    return pl.pallas_call(
        _router_kernel,
        grid=(n // tm,),
        in_specs=[pl.BlockSpec((tm, d), lambda i: (i, 0)),
                  pl.BlockSpec((1, d), lambda i: (0, 0)),
                  pl.BlockSpec((d, 128), lambda i: (0, 0)),
                  pl.BlockSpec((d, 128), lambda i: (0, 0))],
        out_specs=[pl.BlockSpec((tm, 128), lambda i: (i, 0)), pl.BlockSpec((tm, 128), lambda i: (i, 0))],
        out_shape=[jax.ShapeDtypeStruct((n, 128), jnp.int32), jax.ShapeDtypeStruct((n, 128), F32)],
        compiler_params=_cp("parallel"),
        name="router",
    )(x, g.reshape(1, d), wh, wl)


def _row_gather_start(src_hbm, idx_ref, buf, sem, slot, n_rows):
    def body(r, carry):
        pltpu.make_async_copy(src_hbm.at[pl.ds(idx_ref[r], 1), :], buf.at[slot, pl.ds(r, 1), :], sem.at[slot]).start()
        return carry

    lax.fori_loop(0, n_rows, body, 0, unroll=8)


def _row_gather_wait(src_hbm, buf, sem, slot, n_rows):
    pltpu.make_async_copy(src_hbm.at[pl.ds(0, n_rows), :], buf.at[slot], sem.at[slot]).wait()


def _moe_expert_kernel(te_ref, na_ref, src_cur_ref, src_nxt_ref, x_hbm, g_ref, wg_ref, wu_ref, wd_ref, o_ref,
                       buf, xg_ref, acc_ref, sem, *, tm):
    i = pl.program_id(0)
    f = pl.program_id(1)
    n_active = na_ref[0]
    active = i < n_active
    slot = i % 2

    @pl.when((f == 0) & (i == 0) & active)
    def _():
        _row_gather_start(x_hbm, src_cur_ref, buf, sem, 0, tm)

    @pl.when((f == 0) & active)
    def _():
        _row_gather_wait(x_hbm, buf, sem, slot, tm)

        @pl.when(i + 1 < n_active)
        def _():
            _row_gather_start(x_hbm, src_nxt_ref, buf, sem, 1 - slot, tm)

        xg_ref[...] = _rms(buf[slot], g_ref[...]).astype(BF16)
        acc_ref[...] = jnp.zeros_like(acc_ref)

    @pl.when(active)
    def _():
        xn = xg_ref[...]
        hg = _mm(xn, wg_ref[0])
        hu = _mm(xn, wu_ref[0])
        a = (hg * _sigmoid(hg) * hu).astype(BF16)
        acc_ref[...] += _mm(a, wd_ref[0])

    last = f == pl.num_programs(1) - 1

    @pl.when(last & active)
    def _():
        o_ref[...] = acc_ref[...]

    @pl.when(last & jnp.logical_not(active))
    def _():
        o_ref[...] = jnp.zeros_like(o_ref)


def _moe_combine_kernel(pos_cur_ref, pos_nxt_ref, x_ref, w_ref, y_hbm, o_ref, buf, sem, *, tc):
    i = pl.program_id(0)
    slot = i % 2

    @pl.when(i == 0)
    def _():
        _row_gather_start(y_hbm, pos_cur_ref, buf, sem, 0, 2 * tc)

    _row_gather_wait(y_hbm, buf, sem, slot, 2 * tc)

    @pl.when(i + 1 < pl.num_programs(0))
    def _():
        _row_gather_start(y_hbm, pos_nxt_ref, buf, sem, 1 - slot, 2 * tc)

    w = w_ref[...]
    y0 = buf[slot, 0:tc, :]
    y1 = buf[slot, tc:2 * tc, :]
    o_ref[...] = x_ref[...] + w[:, 0:1] * y0 + w[:, 1:2] * y1


MOE_ROW_TILE = 1024
MOE_COMBINE_TILE = 512


def _moe_plan(e_idx, tm, tc):
    n = e_idx.shape[0]
    n_assign = 2 * n
    max_tiles = n_assign // tm + N_EXPERTS
    e_flat = e_idx.reshape(n_assign)
    onehot = (e_flat[:, None] == jnp.arange(N_EXPERTS, dtype=jnp.int32)[None, :]).astype(jnp.int32)
    csum = jnp.cumsum(onehot, axis=0)
    rank = jnp.sum(onehot * csum, axis=1) - 1
    counts = csum[-1]
    padded = ((counts + tm - 1) // tm) * tm
    gend = jnp.cumsum(padded)
    pos = (gend - padded)[e_flat] + rank
    n_active = (gend[-1] // tm).astype(jnp.int32)
    tile_start = jnp.arange(max_tiles, dtype=jnp.int32) * tm
    tile_expert = jnp.sum((tile_start[:, None] >= gend[None, :]).astype(jnp.int32), axis=1)
    last_expert = jnp.sum((tile_start[jnp.maximum(n_active - 1, 0)] >= gend).astype(jnp.int32))
    tile_expert = jnp.where(tile_start < gend[-1], tile_expert, last_expert).astype(jnp.int32)
    src = jnp.zeros((max_tiles * tm,), jnp.int32).at[pos].set(jnp.arange(n_assign, dtype=jnp.int32) // 2)
    pos_tiles = pos.reshape(n // tc, tc, 2).transpose(0, 2, 1).reshape(n_assign)
    return src, tile_expert, n_active.reshape(1), pos_tiles.astype(jnp.int32), max_tiles


def _moe(x, g, w_router, w_gu, w_d):
    n, d = x.shape
    ff = w_d.shape[1]
    tm = min(MOE_ROW_TILE, n)
    tc = min(MOE_COMBINE_TILE, n // 2)
    tf = _pick(ff, (512, 256, 128))
    nf = ff // tf
    idx, wts = _router(x, g, w_router)
    src, tile_expert, n_active, pos_tiles, max_tiles = _moe_plan(idx[:, :2], tm, tc)

    def wmap(col_off):
        def f(i, f_, te, na):
            return (te[i], 0, jnp.where(i < na[0], f_, nf - 1) + col_off)
        return f

    smem = functools.partial(pl.BlockSpec, memory_space=pltpu.SMEM)
    y = pl.pallas_call(
        functools.partial(_moe_expert_kernel, tm=tm),
        grid_spec=pltpu.PrefetchScalarGridSpec(
            num_scalar_prefetch=2,
            grid=(max_tiles, nf),
            in_specs=[smem((tm,), lambda i, f_, te, na: (i,)),
                      smem((tm,), lambda i, f_, te, na: (jnp.minimum(i + 1, max_tiles - 1),)),
                      pl.BlockSpec(memory_space=pl.ANY),
                      pl.BlockSpec((1, d), lambda i, f_, te, na: (0, 0)),
                      pl.BlockSpec((1, d, tf), wmap(0)),
                      pl.BlockSpec((1, d, tf), wmap(nf)),
                      pl.BlockSpec((1, tf, d), lambda i, f_, te, na: (te[i], jnp.where(i < na[0], f_, nf - 1), 0))],
            out_specs=pl.BlockSpec((tm, d), lambda i, f_, te, na: (i, 0)),
            scratch_shapes=[pltpu.VMEM((2, tm, d), F32), pltpu.VMEM((tm, d), BF16), pltpu.VMEM((tm, d), F32),
                            pltpu.SemaphoreType.DMA((2,))]),
        out_shape=jax.ShapeDtypeStruct((max_tiles * tm, d), F32),
        compiler_params=_cp("arbitrary", "arbitrary"),
        name="moe_experts",
    )(tile_expert, n_active, src, src, x, g.reshape(1, d), w_gu, w_gu, w_d)

    n_steps = n // tc
    return pl.pallas_call(
        functools.partial(_moe_combine_kernel, tc=tc),
        grid=(n_steps,),
        in_specs=[smem((2 * tc,), lambda i: (i,)),
                  smem((2 * tc,), lambda i: (jnp.minimum(i + 1, n_steps - 1),)),
                  pl.BlockSpec((tc, d), lambda i: (i, 0)),
                  pl.BlockSpec((tc, 128), lambda i: (i, 0)),
                  pl.BlockSpec(memory_space=pl.ANY)],
        out_specs=pl.BlockSpec((tc, d), lambda i: (i, 0)),
        out_shape=jax.ShapeDtypeStruct((n, d), F32),
        scratch_shapes=[pltpu.VMEM((2, 2 * tc, d), F32), pltpu.SemaphoreType.DMA((2,))],
        compiler_params=_cp("arbitrary"),
        name="moe_combine",
    )(pos_tiles, pos_tiles, x, wts, y)


NA_GROUP = 4
NA_KEYS = NA_WIN_ROWS * GRID_W
NA_BLK = 8 * GRID_W


def _na_bias_table(rpb):
    c = jnp.arange(GRID_W)
    cs = jnp.clip(c - NA_WIN_COLS // 2, 0, GRID_W - NA_WIN_COLS)
    cp = jnp.arange(GRID_W)
    valid = (cp[None, :] >= cs[:, None]) & (cp[None, :] < cs[:, None] + NA_WIN_COLS)
    colrel = jnp.clip(cp[None, :] - c[:, None] + NA_WIN_COLS - 1, 0, 2 * NA_WIN_COLS - 2)
    rr = jnp.arange(NA_WIN_ROWS)[:, None] + jnp.arange(NA_WIN_ROWS)[None, :]
    tbl = rpb.astype(F32)[:, rr][:, :, :, colrel]
    tbl = jnp.where(valid[None, None, None], tbl, NEG_BIG)
    tbl = tbl.reshape(NA_HEADS // NA_GROUP, NA_GROUP, NA_WIN_ROWS, NA_WIN_ROWS, GRID_W, GRID_W)
    tbl = tbl.transpose(2, 0, 3, 5, 1, 4)
    return tbl.reshape(NA_WIN_ROWS, NA_HEADS // NA_GROUP, NA_KEYS, NA_GROUP * GRID_W)


def _na_kernel(q_ref, kp_ref, kc_ref, kn_ref, vp_ref, vc_ref, vn_ref, bias_ref, o_ref, kst, vst, *, rows):
    r = pl.program_id(0) % rows
    i = r // 8

    @pl.when(r % 8 == 0)
    def _():
        kst[0:NA_BLK] = kp_ref[...]
        kst[NA_BLK:2 * NA_BLK] = kc_ref[...]
        kst[2 * NA_BLK:3 * NA_BLK] = kn_ref[...]
        vst[0:NA_BLK] = vp_ref[...]
        vst[NA_BLK:2 * NA_BLK] = vc_ref[...]
        vst[2 * NA_BLK:3 * NA_BLK] = vn_ref[...]

    rs = jnp.clip(r - NA_WIN_ROWS // 2, 0, rows - NA_WIN_ROWS)
    off = pl.multiple_of((rs - 8 * (i - 1)) * GRID_W, GRID_W)
    gw = NA_GROUP * NA_HEAD_DIM
    lane_head = lax.broadcasted_iota(jnp.int32, (GRID_W, gw), 1) // NA_HEAD_DIM
    for grp in range(NA_HEADS // NA_GROUP):
        sl = slice(grp * gw, (grp + 1) * gw)
        qg = q_ref[:, sl]
        bq = jnp.concatenate([jnp.where(lane_head == h, qg, jnp.zeros_like(qg)) for h in range(NA_GROUP)], axis=0)
        kw = kst[pl.ds(off, NA_KEYS), sl]
        st = _mm_nt(kw, bq) * (NA_HEAD_DIM ** -0.5) + bias_ref[0, grp]
        p = jnp.exp(st - jnp.max(st, axis=0, keepdims=True))
        p = (p / jnp.sum(p, axis=0, keepdims=True)).astype(BF16)
        vw = vst[pl.ds(off, NA_KEYS), sl]
        res = _mm_tn(p, vw)
        o = jnp.zeros((GRID_W, gw), F32)
        for h in range(NA_GROUP):
            o = o + jnp.where(lane_head == h, res[h * GRID_W:(h + 1) * GRID_W], 0.0)
        o_ref[:, sl] = o.astype(o_ref.dtype)


def _na_attention(qkv, batch, seq_len, bias_tbl):
    n = qkv.shape[0]
    d = D_MODEL
    rows = seq_len // GRID_W
    assert rows % 8 == 0 and rows >= 16
    nb = rows // 8

    def kmap(delta, col):
        def f(g):
            b = g // rows
            i = (g % rows) // 8
            return (b * nb + jnp.clip(i + delta, 0, nb - 1), col)
        return f

    def bias_map(g):
        r = g % rows
        return (jnp.clip(r - NA_WIN_ROWS // 2, 0, rows - NA_WIN_ROWS) - r + NA_WIN_ROWS - 1, 0, 0, 0)

    kv_specs = [pl.BlockSpec((NA_BLK, d), kmap(dl, col)) for col in (1, 2) for dl in (-1, 0, 1)]
    return pl.pallas_call(
        functools.partial(_na_kernel, rows=rows),
        grid=(batch * rows,),
        in_specs=[pl.BlockSpec((GRID_W, d), lambda g: (g, 0))] + kv_specs
                 + [pl.BlockSpec((1,) + bias_tbl.shape[1:], bias_map)],
        out_specs=pl.BlockSpec((GRID_W, d), lambda g: (g, 0)),
        out_shape=jax.ShapeDtypeStruct((n, d), BF16),
        scratch_shapes=[pltpu.VMEM((3 * NA_BLK, d), BF16), pltpu.VMEM((3 * NA_BLK, d), BF16)],
        compiler_params=_cp("arbitrary"),
        name="na_attention",
    )(qkv, qkv, qkv, qkv, qkv, qkv, qkv, bias_tbl)


def _na_mixer(x, batch, seq_len, g, w_qkv, rpb, w_out):
    qkv = _norm_matmul(x, g, w_qkv)
    o = _na_attention(qkv, batch, seq_len, _na_bias_table(rpb))
    return _proj_res(lambda a: a, [(o, D_MODEL, 0, None)], [], w_out, x, "na_out")


LRU_LANES = 128


def _halo_maps(seq_len, tile, halo, col, tile_of):
    per_tile = tile // halo
    per_seq = seq_len // halo

    def prev(*g):
        return (jnp.maximum(g[0] * per_seq + tile_of(*g) * per_tile - 1, g[0] * per_seq), col)

    def nxt(*g):
        return (jnp.minimum(g[0] * per_seq + (tile_of(*g) + 1) * per_tile, (g[0] + 1) * per_seq - 1), col)

    return prev, nxt


def _lru_kernel(cur_ref, prev_ref, next_ref, cw_ref, cb_ref, wg_ref, ba_ref, bx_ref, lam_ref, o_ref,
                a_scr, b_scr, h_scr, carry, *, tile, n_tiles):
    d = pl.program_id(1)
    i = pl.program_id(2)
    ti = jnp.where(d == 0, i, n_tiles - 1 - i)
    hl = BF16_SUBLANES
    prev = jnp.where(ti == 0, 0.0, prev_ref[...].astype(F32))
    nxt = jnp.where(ti == n_tiles - 1, 0.0, next_ref[...].astype(F32))
    xp = jnp.concatenate([prev, cur_ref[...].astype(F32), nxt], axis=0)
    left = LRU_CONV // 2
    xc = cb_ref[...]
    for k in range(LRU_CONV):
        s = hl + k - left
        xc = xc + xp[s:s + tile] * cw_ref[k:k + 1, :]
    xcb = xc.astype(BF16)
    ga, gx = [], []
    for grp in range(D_MODEL // LRU_LANES):
        gg = _mm(xcb[:, grp * LRU_LANES:(grp + 1) * LRU_LANES], wg_ref[0, grp])
        ga.append(gg[:, :LRU_LANES])
        gx.append(gg[:, LRU_LANES:])
    ga = jnp.concatenate(ga, axis=-1) + ba_ref[0]
    gx = jnp.concatenate(gx, axis=-1) + bx_ref[0]
    log_a = -LRU_C * _sigmoid(ga) * _softplus(-lam_ref[0])
    a = jnp.exp(log_a)
    a_scr[...] = a
    b_scr[...] = jnp.sqrt(1.0 - a * a) * (_sigmoid(gx) * xc)

    @pl.when(i == 0)
    def _():
        carry[...] = jnp.zeros_like(carry)

    def body(t, h):
        tt = jnp.where(d == 0, t, tile - 1 - t)
        h = a_scr[pl.ds(tt, 1), :] * h + b_scr[pl.ds(tt, 1), :]
        h_scr[pl.ds(tt, 1), :] = h
        return h

    carry[...] = lax.fori_loop(0, tile, body, carry[...], unroll=8)
    o_ref[0] = h_scr[...].astype(o_ref.dtype)


def _lru_gate_weights(ga_w, gx_w):
    def bd(w):
        w = w.reshape(2, D_MODEL // LRU_LANES, 2, 64, 64)
        z = jnp.zeros_like(w[:, :, 0])
        top = jnp.concatenate([w[:, :, 0], z], axis=-1)
        bot = jnp.concatenate([z, w[:, :, 1]], axis=-1)
        return jnp.concatenate([top, bot], axis=-2)
    return jnp.concatenate([bd(ga_w), bd(gx_w)], axis=-1).astype(BF16)


def _lru_scan(gb, batch, seq_len, conv_w, conv_b, wg, ba, bx, lam):
    n = gb.shape[0]
    w = D_MODEL
    tile = _pick(seq_len, (512, 256))
    n_tiles = seq_len // tile
    hl = BF16_SUBLANES

    def tile_of(b, d, i):
        return jnp.where(d == 0, i, n_tiles - 1 - i)

    prev_map, next_map = _halo_maps(seq_len, tile, hl, 1, tile_of)
    row = lambda b, d, i: (d, 0, 0)
    return pl.pallas_call(
        functools.partial(_lru_kernel, tile=tile, n_tiles=n_tiles),
        grid=(batch, 2, n_tiles),
        in_specs=[pl.BlockSpec((tile, w), lambda b, d, i: (b * n_tiles + tile_of(b, d, i), 1)),
                  pl.BlockSpec((hl, w), prev_map),
                  pl.BlockSpec((hl, w), next_map),
                  pl.BlockSpec((LRU_CONV, w), lambda b, d, i: (0, 0)),
                  pl.BlockSpec((1, w), lambda b, d, i: (0, 0)),
                  pl.BlockSpec((1,) + wg.shape[1:], lambda b, d, i: (d, 0, 0, 0)),
                  pl.BlockSpec((1, 1, w), row), pl.BlockSpec((1, 1, w), row), pl.BlockSpec((1, 1, w), row)],
        out_specs=pl.BlockSpec((1, tile, w), lambda b, d, i: (d, b * n_tiles + tile_of(b, d, i), 0)),
        out_shape=jax.ShapeDtypeStruct((2, n, w), BF16),
        scratch_shapes=[pltpu.VMEM((tile, w), F32), pltpu.VMEM((tile, w), F32), pltpu.VMEM((tile, w), F32),
                        pltpu.VMEM((1, w), F32)],
        compiler_params=_cp("parallel", "arbitrary", "arbitrary"),
        name="lru_scan",
    )(gb, gb, gb, conv_w, conv_b.reshape(1, w), wg, ba.reshape(2, 1, w), bx.reshape(2, 1, w), lam.reshape(2, 1, w))


def _gelu_tanh(x):
    return 0.5 * x * (1.0 + jnp.tanh(math.sqrt(2.0 / math.pi) * (x + 0.044715 * (x * x * x))))


def _lru_mixer(x, batch, seq_len, g, w_in, conv_w, conv_b, ga_w, ga_b, gx_w, gx_b, lam, w_out):
    gb = _norm_matmul(x, g, w_in)
    h = _lru_scan(gb, batch, seq_len, conv_w, conv_b, _lru_gate_weights(ga_w, gx_w), ga_b, gx_b, lam)

    def prologue(gate, hf, hb):
        return _gelu_tanh(gate.astype(F32)) * (hf.astype(F32) + hb.astype(F32))

    return _proj_res(prologue, [(gb, D_MODEL, 0, None), (h, D_MODEL, 0, 0), (h, D_MODEL, 0, 1)], [], w_out, x,
                     "lru_out")


GLA_EXTRA = 128


def _gla_kernel(q_ref, k_ref, v_ref, lr_ref, w2_ref, gb_ref, o_ref, qb_scr, kb_scr, kd_scr, dec_scr, st_scr,
                *, tile, n_tiles):
    d = pl.program_id(1)
    i = pl.program_id(2)
    c = GLA_CHUNK
    n_chunks = tile // c

    @pl.when(i == 0)
    def _():
        st_scr[...] = jnp.zeros_like(st_scr)

    z = _mm(lr_ref[...], w2_ref[0]) + gb_ref[0]
    g = -_softplus(-z) / GLA_TAU
    row = lax.broadcasted_iota(jnp.int32, (tile, tile), 0)
    col = lax.broadcasted_iota(jnp.int32, (tile, tile), 1)
    same = (row // c) == (col // c)
    fwd = d == 0
    tri = same & (jnp.where(fwd, col, row) <= jnp.where(fwd, row, col))
    tri = jnp.where(tri, 1.0, 0.0).astype(BF16)
    ones = jnp.where(same, 1.0, 0.0).astype(BF16)
    g_hi = g.astype(BF16)
    g_lo = (g - g_hi.astype(F32)).astype(BF16)
    bsum = _mm(tri, g_hi) + _mm(tri, g_lo)
    tot = _mm(ones, g_hi) + _mm(ones, g_lo)
    q = q_ref[...].astype(F32) * (GLA_HK ** -0.5)
    k = k_ref[...].astype(F32)
    qb_scr[...] = (q * jnp.exp(bsum)).astype(BF16)
    kb_scr[...] = (k * jnp.exp(-bsum)).astype(BF16)
    kd_scr[...] = (k * jnp.exp(tot - bsum)).astype(BF16)
    dec_scr[...] = jnp.exp(tot)

    r64 = lax.broadcasted_iota(jnp.int32, (c, c), 0)
    c64 = lax.broadcasted_iota(jnp.int32, (c, c), 1)
    mask = jnp.where(fwd, c64, r64 + 1) <= jnp.where(fwd, r64, c64)
    for cc in range(n_chunks):
        ci = jnp.where(fwd, cc, n_chunks - 1 - cc)
        off = pl.multiple_of(ci * c, c)
        for h in range(GLA_HEADS):
            ks = slice(h * GLA_HK, (h + 1) * GLA_HK)
            vs = slice(h * GLA_HV, (h + 1) * GLA_HV)
            qb = qb_scr[pl.ds(off, c), ks]
            kb = kb_scr[pl.ds(off, c), ks]
            kd = kd_scr[pl.ds(off, c), ks]
            vv = v_ref[pl.ds(off, c), vs]
            att = jnp.where(mask, _mm_nt(qb, kb), 0.0).astype(BF16)
            s_t = st_scr[h]
            o = _mm(att, vv) + _mm_nt(qb, s_t.astype(BF16))
            o_ref[0, pl.ds(off, c), vs] = o.astype(o_ref.dtype)
            dec = dec_scr[pl.ds(off, 1), ks]
            st_scr[h] = s_t * dec + _mm_tn(vv, kd)


def _gla_scan(proj, batch, seq_len, w2p, gate_b):
    n = proj.shape[0]
    tile = _pick(seq_len, (256,))
    n_tiles = seq_len // tile

    def tok(b, d, i):
        return b * n_tiles + jnp.where(d == 0, i, n_tiles - 1 - i)

    return pl.pallas_call(
        functools.partial(_gla_kernel, tile=tile, n_tiles=n_tiles),
        grid=(batch, 2, n_tiles),
        in_specs=[pl.BlockSpec((tile, GLA_DK), lambda b, d, i: (tok(b, d, i), 0)),
                  pl.BlockSpec((tile, GLA_DK), lambda b, d, i: (tok(b, d, i), 1)),
                  pl.BlockSpec((tile, GLA_DV), lambda b, d, i: (tok(b, d, i), 1)),
                  pl.BlockSpec((tile, GLA_EXTRA), lambda b, d, i: (tok(b, d, i), (2 * GLA_DK + 2 * GLA_DV) // GLA_EXTRA)),
                  pl.BlockSpec((1, GLA_EXTRA, GLA_DK), lambda b, d, i: (d, 0, 0)),
                  pl.BlockSpec((1, 1, GLA_DK), lambda b, d, i: (d, 0, 0))],
        out_specs=pl.BlockSpec((1, tile, GLA_DV), lambda b, d, i: (d, tok(b, d, i), 0)),
        out_shape=jax.ShapeDtypeStruct((2, n, GLA_DV), BF16),
        scratch_shapes=[pltpu.VMEM((tile, GLA_DK), BF16), pltpu.VMEM((tile, GLA_DK), BF16),
                        pltpu.VMEM((tile, GLA_DK), BF16), pltpu.VMEM((tile, GLA_DK), F32),
                        pltpu.VMEM((GLA_HEADS, GLA_HV, GLA_HK), F32)],
        compiler_params=_cp("parallel", "arbitrary", "arbitrary"),
        name="gla_scan",
    )(proj, proj, proj, proj, w2p, gate_b.reshape(2, 1, GLA_DK))


def _gla_mixer(x, batch, seq_len, g, w_in, gate_w1, gate_w2, gate_b, head_norm, w_out):
    d = D_MODEL
    pad = jnp.zeros((d, GLA_EXTRA - 2 * GLA_GATE_RANK), w_in.dtype)
    w_ext = jnp.concatenate([w_in, gate_w1[0].astype(w_in.dtype), gate_w1[1].astype(w_in.dtype), pad], axis=1)
    proj = _norm_matmul(x, g, w_ext)
    w2p = jnp.zeros((2, GLA_EXTRA, GLA_DK), F32)
    w2p = w2p.at[0, :GLA_GATE_RANK].set(gate_w2[0]).at[1, GLA_GATE_RANK:2 * GLA_GATE_RANK].set(gate_w2[1])
    o = _gla_scan(proj, batch, seq_len, w2p.astype(BF16), gate_b)

    def prologue(of, ob, r, hn):
        o_sum = of.astype(F32) + ob.astype(F32)
        parts = []
        for h in range(GLA_HEADS):
            oh = o_sum[:, h * GLA_HV:(h + 1) * GLA_HV]
            parts.append(_rms(oh, hn))
        r = r.astype(F32)
        return jnp.concatenate(parts, axis=-1) * (r * _sigmoid(r))

    return _proj_res(prologue, [(o, GLA_DV, 0, 0), (o, GLA_DV, 0, 1), (proj, GLA_DV, 2, None)],
                     [head_norm.reshape(1, GLA_HV).astype(F32)], w_out, x, "gla_out")


HY_PAD = 128


def _hy_filter_kernel(bands_ref, w1t_ref, w1c_ref, w1s_ref, b1_ref, w2_ref, b2_ref, w3_ref, freq_ref, ld_ref,
                      h_ref, s_ref, *, seq_len, tile):
    i = pl.program_id(0)
    hp = lax.Precision.HIGHEST
    j = (i * tile + lax.broadcasted_iota(jnp.int32, (tile, 1), 0)).astype(F32)
    t = j / seq_len
    ang = (2.0 * math.pi / seq_len) * j * bands_ref[...]
    pre = (t * w1t_ref[...]
           + jnp.dot(jnp.cos(ang), w1c_ref[...], precision=hp, preferred_element_type=F32)
           + jnp.dot(jnp.sin(-ang), w1s_ref[...], precision=hp, preferred_element_type=F32)
           + b1_ref[...])
    h = jnp.sin(freq_ref[0:1, :] * pre)
    h = jnp.sin(freq_ref[1:2, :] * (jnp.dot(h, w2_ref[...], precision=hp, preferred_element_type=F32) + b2_ref[...]))
    h = jnp.dot(h, w3_ref[...], precision=hp, preferred_element_type=F32)
    dist = jnp.abs(j - (seq_len // 2)) / (seq_len / 2)
    h = h * jnp.exp(-dist * jnp.exp(ld_ref[...]))
    h_ref[...] = h.astype(h_ref.dtype)

    @pl.when(i == 0)
    def _():
        s_ref[...] = jnp.zeros_like(s_ref)

    s_ref[...] += jnp.sum(jnp.abs(h), axis=0, keepdims=True)


def _hy_filter(seq_len, w1, b1, w2, b2, w3, freq, log_decay):
    hid = HY_FILTER_HID
    c2 = w3.shape[1]
    tile = _pick(seq_len, (512, 256))
    bands = jnp.zeros((1, HY_PAD), F32).at[0, :HY_BANDS].set(jnp.linspace(1e-4, HY_BANDS - 1, HY_BANDS, dtype=F32))
    w1 = w1.astype(F32)
    w1c = jnp.zeros((HY_PAD, hid), F32).at[:HY_BANDS].set(w1[1:1 + HY_BANDS])
    w1s = jnp.zeros((HY_PAD, hid), F32).at[:HY_BANDS].set(w1[1 + HY_BANDS:])
    full = lambda shape: pl.BlockSpec(shape, lambda i: (0, 0))
    return pl.pallas_call(
        functools.partial(_hy_filter_kernel, seq_len=seq_len, tile=tile),
        grid=(seq_len // tile,),
        in_specs=[full((1, HY_PAD)), full((1, hid)), full((HY_PAD, hid)), full((HY_PAD, hid)), full((1, hid)),
                  full((hid, hid)), full((1, hid)), full((hid, c2)), full((2, hid)), full((1, c2))],
        out_specs=[pl.BlockSpec((tile, c2), lambda i: (i, 0)), pl.BlockSpec((1, c2), lambda i: (0, 0))],
        out_shape=[jax.ShapeDtypeStruct((seq_len, c2), BF16), jax.ShapeDtypeStruct((1, c2), F32)],
        compiler_params=_cp("arbitrary"),
        name="hy_filter",
    )(bands, w1[0:1], w1c, w1s, b1.reshape(1, hid).astype(F32), w2.astype(F32), b2.reshape(1, hid).astype(F32),
      w3.astype(F32), freq.astype(F32), log_decay.reshape(1, c2).astype(F32))


def _hy_pre_kernel(cur_ref, prev_ref, next_ref, w_ref, b_ref, o_ref, *, tile, n_tiles):
    i = pl.program_id(1)
    hl = BF16_SUBLANES
    prev = jnp.where(i == 0, 0.0, prev_ref[...].astype(F32))
    nxt = jnp.where(i == n_tiles - 1, 0.0, next_ref[...].astype(F32))
    xp = jnp.concatenate([prev, cur_ref[...].astype(F32), nxt], axis=0)
    y = b_ref[...]
    for k in range(3):
        s = hl + k - 1
        y = y + xp[s:s + tile] * w_ref[k:k + 1, :]
    o_ref[...] = y.astype(o_ref.dtype)


def _hy_pre(u, batch, seq_len, short_w, short_b):
    n, c3 = u.shape
    w = D_MODEL
    tile = _pick(seq_len, (512, 256))
    n_tiles = seq_len // tile
    hl = BF16_SUBLANES
    per_tile, per_seq = tile // hl, seq_len // hl
    return pl.pallas_call(
        functools.partial(_hy_pre_kernel, tile=tile, n_tiles=n_tiles),
        grid=(batch, n_tiles, c3 // w),
        in_specs=[pl.BlockSpec((tile, w), lambda b, i, c: (b * n_tiles + i, c)),
                  pl.BlockSpec((hl, w), lambda b, i, c: (jnp.maximum(b * per_seq + i * per_tile - 1, b * per_seq), c)),
                  pl.BlockSpec((hl, w), lambda b, i, c: (jnp.minimum(b * per_seq + (i + 1) * per_tile,
                                                                     (b + 1) * per_seq - 1), c)),
                  pl.BlockSpec((3, w), lambda b, i, c: (0, c)),
                  pl.BlockSpec((1, w), lambda b, i, c: (0, c))],
        out_specs=pl.BlockSpec((tile, w), lambda b, i, c: (b * n_tiles + i, c)),
        out_shape=jax.ShapeDtypeStruct((n, c3), BF16),
        compiler_params=_cp("parallel", "parallel", "parallel"),
        name="hy_pre",
    )(u, u, u, short_w.astype(F32), short_b.reshape(1, c3).astype(F32))


def _fft_plan(seq_len):
    n = 2 * seq_len
    n2 = 256 if n >= 32768 else 128
    n1 = n // n2
    assert n1 % 4 == 0 and n1 * n2 == n
    return n, n1, n2, n1 // 2


def _fft_tables(seq_len):
    n, n1, n2, ah = _fft_plan(seq_len)
    b = jnp.arange(n2, dtype=jnp.int32)
    ka = jnp.arange(n1, dtype=jnp.int32)
    a = jnp.arange(ah, dtype=jnp.int32)
    w = 2.0 * math.pi / n
    m = (ka[None, :, None] * (b[:, None, None] + n2 * a[None, None, :])) % n
    ang = w * m.astype(F32)
    ft = jnp.concatenate([jnp.cos(ang), -jnp.sin(ang)], axis=1).astype(BF16)
    ap = a + n1 // 4
    m = ((b[:, None, None] + n2 * ap[None, :, None]) * ka[None, None, :]) % n
    ang = w * m.astype(F32)
    fti = (jnp.concatenate([jnp.cos(ang), -jnp.sin(ang)], axis=2) / n).astype(BF16)
    ang = (2.0 * math.pi / n2) * ((b[:, None] * b[None, :]) % n2).astype(F32)
    fr, fi = jnp.cos(ang), -jnp.sin(ang)
    m2 = jnp.block([[fr, -fi], [fi, fr]]).astype(BF16)
    m2i = jnp.block([[fr, fi], [-fi, fr]]).astype(BF16)
    return ft, fti, m2, m2i


def _fft_first_kernel(z_ref, ft_ref, o_ref, *, tb):
    for i in range(tb):
        o_ref[0, i] = _mm(ft_ref[i], z_ref[0, i]).astype(o_ref.dtype)


def _fft_first(zb, ft):
    bsz, n2, ah, c = zb.shape
    rows = ft.shape[1]
    cb = 1024
    tb = _pick(n2, (16, 8))
    return pl.pallas_call(
        functools.partial(_fft_first_kernel, tb=tb),
        grid=(bsz, n2 // tb, c // cb),
        in_specs=[pl.BlockSpec((1, tb, ah, cb), lambda b, j, cc: (b, j, 0, cc)),
                  pl.BlockSpec((tb, rows, ah), lambda b, j, cc: (j, 0, 0))],
        out_specs=pl.BlockSpec((1, tb, rows, cb), lambda b, j, cc: (b, j, 0, cc)),
        out_shape=jax.ShapeDtypeStruct((bsz, n2, rows, c), BF16),
        compiler_params=_cp("parallel", "parallel", "parallel"),
        name="fft_first",
    )(zb, ft)


def _fft_spectrum_kernel(a_ref, s_ref, m_ref, o_ref, *, tk):
    for i in range(tk):
        o_ref[0, i] = (_mm(m_ref[...], a_ref[0, i]) / s_ref[...]).astype(o_ref.dtype)


def _fft_conv_kernel(a_ref, h_ref, m_ref, mi_ref, o_ref, *, tk, n2):
    for i in range(tk):
        x = _mm(m_ref[...], a_ref[0, i])
        xr, xi = x[:n2], x[n2:]
        hr, hi = h_ref[0, i, :n2].astype(F32), h_ref[0, i, n2:].astype(F32)
        p = jnp.concatenate([xr * hr - xi * hi, xr * hi + xi * hr], axis=0).astype(BF16)
        o_ref[0, i] = _mm(mi_ref[...], p).astype(o_ref.dtype)


def _fft_mid(ak, m2, m2i=None, spec=None, spec_col=0, scale=None):
    bsz, n1, r2, c = ak.shape
    cb = 1024
    tk = _pick(n1, (4, 2))
    grid = (bsz, n1 // tk, c // cb)
    a_spec = pl.BlockSpec((1, tk, r2, cb), lambda b, k, cc: (b, k, 0, cc))
    m_spec = pl.BlockSpec((r2, r2), lambda b, k, cc: (0, 0))
    if spec is None:
        return pl.pallas_call(
            functools.partial(_fft_spectrum_kernel, tk=tk),
            grid=grid,
            in_specs=[a_spec, pl.BlockSpec((1, cb), lambda b, k, cc: (0, cc)), m_spec],
            out_specs=a_spec,
            out_shape=jax.ShapeDtypeStruct(ak.shape, BF16),
            compiler_params=_cp("parallel", "parallel", "parallel"),
            name="fft_spectrum",
        )(ak, scale, m2)
    return pl.pallas_call(
        functools.partial(_fft_conv_kernel, tk=tk, n2=r2 // 2),
        grid=grid,
        in_specs=[a_spec, pl.BlockSpec((1, tk, r2, cb), lambda b, k, cc: (0, k, 0, spec_col + cc)), m_spec, m_spec],
        out_specs=a_spec,
        out_shape=jax.ShapeDtypeStruct(ak.shape, BF16),
        compiler_params=_cp("parallel", "parallel", "parallel"),
        name="fft_conv",
    )(ak, spec, m2, m2i)


def _fft_last_kernel(c_ref, fti_ref, o_ref, *, tb):
    for i in range(tb):
        o_ref[0, i] = _mm(fti_ref[i], c_ref[0, i]).astype(o_ref.dtype)


def _fft_last(cb_arr, fti):
    bsz, n2, rows, c = cb_arr.shape
    ah = fti.shape[1]
    cb = 1024
    tb = _pick(n2, (16, 8))
    return pl.pallas_call(
        functools.partial(_fft_last_kernel, tb=tb),
        grid=(bsz, n2 // tb, c // cb),
        in_specs=[pl.BlockSpec((1, tb, rows, cb), lambda b, j, cc: (b, j, 0, cc)),
                  pl.BlockSpec((tb, ah, rows), lambda b, j, cc: (j, 0, 0))],
        out_specs=pl.BlockSpec((1, tb, ah, cb), lambda b, j, cc: (b, j, 0, cc)),
        out_shape=jax.ShapeDtypeStruct((bsz, n2, ah, c), BF16),
        compiler_params=_cp("parallel", "parallel", "parallel"),
        name="fft_last",
    )(cb_arr, fti)


def _to_b_major(z, bsz, n2, ah):
    c = z.shape[-1]
    return z.reshape(bsz, ah, n2, c).transpose(0, 2, 1, 3)


def _swap_digits(arr, inner):
    bsz, p, r, c = arr.shape
    return arr.reshape(bsz, p, 2, inner, c).transpose(0, 3, 2, 1, 4).reshape(bsz, inner, 2 * p, c)


def _fft_forward(z, bsz, seq_len, ft):
    _, n1, n2, ah = _fft_plan(seq_len)
    a = _fft_first(_to_b_major(z, bsz, n2, ah), ft)
    return _swap_digits(a, n1)


def _long_conv(z, bsz, seq_len, tables, spec, order):
    ft, fti, m2, m2i = tables
    _, n1, n2, ah = _fft_plan(seq_len)
    ak = _fft_forward(z, bsz, seq_len, ft)
    ck = _fft_mid(ak, m2, m2i, spec=spec, spec_col=order * (D_MODEL // 1024))
    y = _fft_last(_swap_digits(ck, n2), fti)
    return y.transpose(0, 2, 1, 3).reshape(bsz * seq_len, z.shape[-1])


def _hy_gate_kernel(x1_ref, y_ref, v_ref, skip_ref, o_ref):
    v = v_ref[...].astype(F32)
    o_ref[...] = (x1_ref[...].astype(F32) * (y_ref[...].astype(F32) + skip_ref[...] * v)).astype(o_ref.dtype)


def _hy_gate(u3, y, skip0):
    n, w = y.shape
    tm = _pick(n, (1024, 512, 256))
    return pl.pallas_call(
        _hy_gate_kernel,
        grid=(n // tm,),
        in_specs=[pl.BlockSpec((tm, w), lambda i: (i, 1)), pl.BlockSpec((tm, w), lambda i: (i, 0)),
                  pl.BlockSpec((tm, w), lambda i: (i, 0)), pl.BlockSpec((1, w), lambda i: (0, 0))],
        out_specs=pl.BlockSpec((tm, w), lambda i: (i, 0)),
        out_shape=jax.ShapeDtypeStruct((n, w), BF16),
        compiler_params=_cp("parallel"),
        name="hy_gate",
    )(u3, y, u3, skip0.reshape(1, w).astype(F32))


def _hyena_mixer(x, batch, seq_len, g, w_in, short_w, short_b, fw1, fb1, fw2, fb2, fw3, ffreq, log_decay, skip,
                 w_out):
    d = D_MODEL
    tables = _fft_tables(seq_len)
    filt, l1 = _hy_filter(seq_len, fw1, fb1, fw2, fb2, fw3, ffreq, log_decay)
    fk = _fft_forward(filt, 1, seq_len, tables[0])
    spec = _fft_mid(fk, tables[2], scale=l1)
    u3 = _hy_pre(_norm_matmul(x, g, w_in), batch, seq_len, short_w, short_b)
    y1 = _long_conv(u3[:, :d], batch, seq_len, tables, spec, 0)
    z1 = _hy_gate(u3, y1, skip[0])
    y2 = _long_conv(z1, batch, seq_len, tables, spec, 1)

    def prologue(x2, yy, zz, sk):
        return x2.astype(F32) * (yy.astype(F32) + sk * zz.astype(F32))

    return _proj_res(prologue, [(u3, d, 2, None), (y2, d, 0, None), (z1, d, 0, None)],
                     [skip[1].reshape(1, d).astype(F32)], w_out, x, "hy_out")


def _trunk(x, mem, p):
    batch, seq_len, d = x.shape
    x = x.reshape(batch * seq_len, d)
    mem = mem.reshape(batch * MEM_TOKENS, d)
    depth = p["norm_mix"].shape[0]
    for i in range(depth):
        m, j = i % 4, i // 4
        g = p["norm_mix"][i]
        if m == 0:
            x = _na_mixer(x, batch, seq_len, g, p["na_w_qkv"][j], p["na_rpb"][j], p["na_w_out"][j])
        elif m == 1:
            x = _lru_mixer(x, batch, seq_len, g, p["lru_w_in"][j], p["lru_conv_w"][j], p["lru_conv_b"][j],
                           p["lru_gate_a_w"][j], p["lru_gate_a_b"][j], p["lru_gate_x_w"][j], p["lru_gate_x_b"][j],
                           p["lru_lambda"][j], p["lru_w_out"][j])
        elif m == 2:
            x = _gla_mixer(x, batch, seq_len, g, p["gla_w_in"][j], p["gla_gate_w1"][j], p["gla_gate_w2"][j],
                           p["gla_gate_b"][j], p["gla_head_norm"][j], p["gla_w_out"][j])
        else:
            x = _hyena_mixer(x, batch, seq_len, g, p["hy_w_in"][j], p["hy_short_w"][j], p["hy_short_b"][j],
                             p["hy_filt_w1"][j], p["hy_filt_b1"][j], p["hy_filt_w2"][j], p["hy_filt_b2"][j],
                             p["hy_filt_w3"][j], p["hy_filt_freq"][j], p["hy_log_decay"][j], p["hy_skip"][j],
                             p["hy_w_out"][j])
        kv = _norm_matmul(mem, p["norm_mem"][i], p["xa_w_kv"][i])
        x = _xattn(x, seq_len, p["norm_xattn"][i], p["xa_w_q"][i], kv, p["xa_w_out"][i])
        if i % 2 == 0:
            x = _ffn(x, p["norm_ffn"][i], p["ffn_w_gate_up"][i // 2], p["ffn_w_down"][i // 2])
        else:
            x = _moe(x, p["norm_ffn"][i], p["moe_router"][i // 2], p["moe_w_gate_up"][i // 2], p["moe_w_down"][i // 2])
    return _final_norm(x, p["norm_final"]).reshape(batch, seq_len, d)


_MATMUL_WEIGHTS = ("na_w_qkv", "na_w_out", "lru_w_in", "lru_w_out", "gla_w_in", "gla_w_out", "hy_w_in", "hy_w_out",
                   "xa_w_q", "xa_w_kv", "xa_w_out", "ffn_w_gate_up", "ffn_w_down", "moe_w_gate_up", "moe_w_down")


def kernel(x_prompt, x_sample, mem_prompt, mem_sample, norm_mix, norm_xattn, norm_mem, norm_ffn, norm_final, na_w_qkv, na_rpb, na_w_out, lru_w_in, lru_conv_w, lru_conv_b, lru_gate_a_w, lru_gate_a_b, lru_gate_x_w, lru_gate_x_b, lru_lambda, lru_w_out, gla_w_in, gla_gate_w1, gla_gate_w2, gla_gate_b, gla_head_norm, gla_w_out, hy_w_in, hy_short_w, hy_short_b, hy_filt_w1, hy_filt_b1, hy_filt_w2, hy_filt_b2, hy_filt_w3, hy_filt_freq, hy_log_decay, hy_skip, hy_w_out, xa_w_q, xa_w_kv, xa_w_out, ffn_w_gate_up, ffn_w_down, moe_router, moe_w_gate_up, moe_w_down):
    p = dict(norm_mix=norm_mix, norm_xattn=norm_xattn, norm_mem=norm_mem, norm_ffn=norm_ffn, norm_final=norm_final,
             na_w_qkv=na_w_qkv, na_rpb=na_rpb, na_w_out=na_w_out,
             lru_w_in=lru_w_in, lru_conv_w=lru_conv_w, lru_conv_b=lru_conv_b, lru_gate_a_w=lru_gate_a_w,
             lru_gate_a_b=lru_gate_a_b, lru_gate_x_w=lru_gate_x_w, lru_gate_x_b=lru_gate_x_b, lru_lambda=lru_lambda,
             lru_w_out=lru_w_out,
             gla_w_in=gla_w_in, gla_gate_w1=gla_gate_w1, gla_gate_w2=gla_gate_w2, gla_gate_b=gla_gate_b,
             gla_head_norm=gla_head_norm, gla_w_out=gla_w_out,
             hy_w_in=hy_w_in, hy_short_w=hy_short_w, hy_short_b=hy_short_b, hy_filt_w1=hy_filt_w1,
             hy_filt_b1=hy_filt_b1, hy_filt_w2=hy_filt_w2, hy_filt_b2=hy_filt_b2, hy_filt_w3=hy_filt_w3,
             hy_filt_freq=hy_filt_freq, hy_log_decay=hy_log_decay, hy_skip=hy_skip, hy_w_out=hy_w_out,
             xa_w_q=xa_w_q, xa_w_kv=xa_w_kv, xa_w_out=xa_w_out,
             ffn_w_gate_up=ffn_w_gate_up, ffn_w_down=ffn_w_down,
             moe_router=moe_router, moe_w_gate_up=moe_w_gate_up, moe_w_down=moe_w_down)
    for name in _MATMUL_WEIGHTS:
        p[name] = p[name].astype(BF16)
    return (_trunk(x_prompt, mem_prompt, p), _trunk(x_sample, mem_sample, p))
```

```python
import functools
import math

import jax
import jax.numpy as jnp
from jax import lax
from jax.experimental import pallas as pl
from jax.experimental.pallas import tpu as pltpu

F32 = jnp.float32
BF16 = jnp.bfloat16

D_MODEL = 1024
RMS_EPS = 1e-6
GRID_W = 64
NA_HEADS = 16
NA_HEAD_DIM = 64
NA_WIN_ROWS = 8
NA_WIN_COLS = 16
LRU_C = 8.0
LRU_CONV = 4
GLA_HEADS = 4
GLA_DK = 512
GLA_DV = 1024
GLA_HK = 128
GLA_HV = 256
GLA_GATE_RANK = 16
GLA_TAU = 16.0
GLA_CHUNK = 64
HY_BANDS = 16
HY_FILTER_HID = 64
MEM_TOKENS = 256
XA_HEADS = 4
XA_HEAD_DIM = 256
D_FF = 3584
N_EXPERTS = 8
NEG_BIG = -1e30

V7X_VMEM_BYTES = 64 * 1024 * 1024
VMEM_LIMIT = V7X_VMEM_BYTES - 8 * 1024 * 1024
BF16_SUBLANES = 16


def _cp(*sem):
    return pltpu.CompilerParams(dimension_semantics=sem, vmem_limit_bytes=VMEM_LIMIT)


def _pick(n, cands):
    for c in cands:
        if n % c == 0:
            return c
    return n


def _rms(x, g):
    return x * lax.rsqrt(jnp.mean(x * x, axis=-1, keepdims=True) + RMS_EPS) * g


def _sigmoid(x):
    return 1.0 / (1.0 + jnp.exp(-x))


def _softplus(x):
    return jnp.maximum(x, 0.0) + jnp.log1p(jnp.exp(-jnp.abs(x)))


def _mm(a, b):
    return jnp.dot(a, b, preferred_element_type=F32)


def _mm_nt(a, b):
    return lax.dot_general(a, b, (((1,), (1,)), ((), ())), preferred_element_type=F32)


def _mm_tn(a, b):
    return lax.dot_general(a, b, (((0,), (0,)), ((), ())), preferred_element_type=F32)


def _norm_matmul_kernel(x_ref, g_ref, w_ref, o_ref, xn_ref):
    @pl.when(pl.program_id(1) == 0)
    def _():
        xn_ref[...] = _rms(x_ref[...], g_ref[...]).astype(BF16)

    o_ref[...] = _mm(xn_ref[...], w_ref[...]).astype(o_ref.dtype)


def _norm_matmul(x, g, w, out_dtype=BF16):
    n, d = x.shape
    f = w.shape[1]
    tm = _pick(n, (1024, 512, 256))
    tn = _pick(f, (1024, 640, 512, 256, 128))
    return pl.pallas_call(
        _norm_matmul_kernel,
        grid=(n // tm, f // tn),
        in_specs=[pl.BlockSpec((tm, d), lambda i, j: (i, 0)),
                  pl.BlockSpec((1, d), lambda i, j: (0, 0)),
                  pl.BlockSpec((d, tn), lambda i, j: (0, j))],
        out_specs=pl.BlockSpec((tm, tn), lambda i, j: (i, j)),
        out_shape=jax.ShapeDtypeStruct((n, f), out_dtype),
        scratch_shapes=[pltpu.VMEM((tm, d), BF16)],
        compiler_params=_cp("parallel", "arbitrary"),
        name="norm_matmul",
    )(x, g.reshape(1, d), w)


def _final_norm_kernel(x_ref, g_ref, o_ref):
    o_ref[...] = _rms(x_ref[...], g_ref[...])


def _final_norm(x, g):
    n, d = x.shape
    tm = _pick(n, (1024, 512, 256))
    return pl.pallas_call(
        _final_norm_kernel,
        grid=(n // tm,),
        in_specs=[pl.BlockSpec((tm, d), lambda i: (i, 0)), pl.BlockSpec((1, d), lambda i: (0, 0))],
        out_specs=pl.BlockSpec((tm, d), lambda i: (i, 0)),
        out_shape=jax.ShapeDtypeStruct((n, d), F32),
        compiler_params=_cp("parallel"),
        name="final_norm",
    )(x, g.reshape(1, d))


def _proj_res(prologue, tok_inputs, row_inputs, w, res, name):
    n, d = res.shape
    k = w.shape[0]
    tm = _pick(n, (512, 256))
    n_tok, n_row = len(tok_inputs), len(row_inputs)

    def kern(*refs):
        toks = []
        for r, (_, _, _, lead) in zip(refs[:n_tok], tok_inputs):
            toks.append(r[0] if lead is not None else r[...])
        rows = [r[...] for r in refs[n_tok:n_tok + n_row]]
        w_ref, res_ref, o_ref = refs[n_tok + n_row:]
        a = prologue(*toks, *rows)
        o_ref[...] = res_ref[...] + _mm(a.astype(BF16), w_ref[...])

    in_specs, args = [], []
    for arr, width, col, lead in tok_inputs:
        if lead is None:
            in_specs.append(pl.BlockSpec((tm, width), lambda i, c=col: (i, c)))
        else:
            in_specs.append(pl.BlockSpec((1, tm, width), lambda i, c=col, l=lead: (l, i, c)))
        args.append(arr)
    for arr in row_inputs:
        in_specs.append(pl.BlockSpec(arr.shape, lambda i: (0, 0)))
        args.append(arr)
    in_specs += [pl.BlockSpec((k, d), lambda i: (0, 0)), pl.BlockSpec((tm, d), lambda i: (i, 0))]
    args += [w, res]
    return pl.pallas_call(
        kern,
        grid=(n // tm,),
        in_specs=in_specs,
        out_specs=pl.BlockSpec((tm, d), lambda i: (i, 0)),
        out_shape=jax.ShapeDtypeStruct((n, d), F32),
        compiler_params=_cp("parallel"),
        name=name,
    )(*args)


def _xattn_kernel(x_ref, g_ref, wq_ref, k_ref, v_ref, wo_ref, o_ref):
    x = x_ref[...]
    xn = _rms(x, g_ref[...]).astype(BF16)
    q = (_mm(xn, wq_ref[...]) * (XA_HEAD_DIM ** -0.5)).astype(BF16)
    outs = []
    for h in range(XA_HEADS):
        sl = slice(h * XA_HEAD_DIM, (h + 1) * XA_HEAD_DIM)
        s = _mm_nt(q[:, sl], k_ref[:, sl])
        p = jnp.exp(s - jnp.max(s, axis=-1, keepdims=True))
        p = p / jnp.sum(p, axis=-1, keepdims=True)
        outs.append(_mm(p.astype(BF16), v_ref[:, sl]))
    o = jnp.concatenate(outs, axis=-1).astype(BF16)
    o_ref[...] = x + _mm(o, wo_ref[...])


def _xattn(x, seq_len, g, wq, kv, wo):
    n, d = x.shape
    tm = _pick(seq_len, (512, 256))
    per_b = seq_len // tm
    m = MEM_TOKENS
    return pl.pallas_call(
        _xattn_kernel,
        grid=(n // tm,),
        in_specs=[pl.BlockSpec((tm, d), lambda i: (i, 0)),
                  pl.BlockSpec((1, d), lambda i: (0, 0)),
                  pl.BlockSpec((d, d), lambda i: (0, 0)),
                  pl.BlockSpec((m, d), lambda i: (i // per_b, 0)),
                  pl.BlockSpec((m, d), lambda i: (i // per_b, 1)),
                  pl.BlockSpec((d, d), lambda i: (0, 0))],
        out_specs=pl.BlockSpec((tm, d), lambda i: (i, 0)),
        out_shape=jax.ShapeDtypeStruct((n, d), F32),
        compiler_params=_cp("parallel"),
        name="xattn",
    )(x, g.reshape(1, d), wq, kv, kv, wo)


def _ffn_kernel(x_ref, g_ref, wg_ref, wu_ref, wd_ref, o_ref, xn_ref, acc_ref):
    f = pl.program_id(1)

    @pl.when(f == 0)
    def _():
        xn_ref[...] = _rms(x_ref[...], g_ref[...]).astype(BF16)
        acc_ref[...] = jnp.zeros_like(acc_ref)

    xn = xn_ref[...]
    hg = _mm(xn, wg_ref[...])
    hu = _mm(xn, wu_ref[...])
    a = (hg * _sigmoid(hg) * hu).astype(BF16)
    acc_ref[...] += _mm(a, wd_ref[...])

    @pl.when(f == pl.num_programs(1) - 1)
    def _():
        o_ref[...] = x_ref[...] + acc_ref[...]


def _ffn(x, g, w_gu, w_d):
    n, d = x.shape
    ff = w_d.shape[0]
    tm = _pick(n, (1024, 512, 256))
    tf = _pick(ff, (896, 512, 256, 128))
    nf = ff // tf
    return pl.pallas_call(
        _ffn_kernel,
        grid=(n // tm, nf),
        in_specs=[pl.BlockSpec((tm, d), lambda i, f: (i, 0)),
                  pl.BlockSpec((1, d), lambda i, f: (0, 0)),
                  pl.BlockSpec((d, tf), lambda i, f: (0, f)),
                  pl.BlockSpec((d, tf), lambda i, f: (0, f + nf)),
                  pl.BlockSpec((tf, d), lambda i, f: (f, 0))],
        out_specs=pl.BlockSpec((tm, d), lambda i, f: (i, 0)),
        out_shape=jax.ShapeDtypeStruct((n, d), F32),
        scratch_shapes=[pltpu.VMEM((tm, d), BF16), pltpu.VMEM((tm, d), F32)],
        compiler_params=_cp("parallel", "arbitrary"),
        name="ffn",
    )(x, g.reshape(1, d), w_gu, w_gu, w_d)


def _router_kernel(x_ref, g_ref, wh_ref, wl_ref, idx_ref, wts_ref):
    xn = _rms(x_ref[...], g_ref[...])
    xh = xn.astype(BF16)
    xl = (xn - xh.astype(F32)).astype(BF16)
    logits = _mm(xh, wh_ref[...]) + _mm(xh, wl_ref[...]) + _mm(xl, wh_ref[...])
    lane = lax.broadcasted_iota(jnp.int32, logits.shape, 1)
    logits = jnp.where(lane < N_EXPERTS, logits, NEG_BIG)
    m1 = jnp.max(logits, axis=-1, keepdims=True)
    i1 = jnp.min(jnp.where(logits == m1, lane, 128), axis=-1, keepdims=True)
    rest = jnp.where(lane == i1, NEG_BIG, logits)
    m2 = jnp.max(rest, axis=-1, keepdims=True)
    i2 = jnp.min(jnp.where(rest == m2, lane, 128), axis=-1, keepdims=True)
    e2 = jnp.exp(m2 - m1)
    den = 1.0 + e2
    idx_ref[...] = jnp.where(lane == 0, i1, jnp.where(lane == 1, i2, 0))
    wts_ref[...] = jnp.where(lane == 0, 1.0 / den, jnp.where(lane == 1, e2 / den, 0.0))


def _router(x, g, w_router):
    n, d = x.shape
    tm = _pick(n, (1024, 512, 256))
    wp = jnp.zeros((d, 128), F32).at[:, :N_EXPERTS].set(w_router)
    wh = wp.astype(BF16)
    wl = (wp - wh.astype(F32)).astype(BF16)
    return pl.pallas_call(
        _router_kernel,
        grid=(n // tm,),
        in_specs=[pl.BlockSpec((tm, d), lambda i: (i, 0)),
                  pl.BlockSpec((1, d), lambda i: (0, 0)),
                  pl.BlockSpec((d, 128), lambda i: (0, 0)),
                  pl.BlockSpec((d, 128), lambda i: (0, 0))],
        out_specs=[pl.BlockSpec((tm, 128), lambda i: (i, 0)), pl.BlockSpec((tm, 128), lambda i: (i, 0))],
        out_shape=[jax.ShapeDtypeStruct((n, 128), jnp.int32), jax.ShapeDtypeStruct((n, 128), F32)],
        compiler_params=_cp("parallel"),
        name="router",
    )(x, g.reshape(1, d), wh, wl)


F32_SUBLANES = 8


def _row_copy(src_hbm, row, buf, sem, slot, grp, sub):
    return pltpu.make_async_copy(src_hbm.at[pl.ds(row, 1), :], buf.at[slot, grp, pl.ds(sub, 1), :], sem.at[slot])


def _row_gather_groups(src_hbm, idx_ref, buf, sem, slot, first_grp, n_groups):
    for g in range(n_groups):
        for sub in range(F32_SUBLANES):
            row = idx_ref[(first_grp + g) * F32_SUBLANES + sub]
            _row_copy(src_hbm, row, buf, sem, slot, first_grp + g, sub).start()


def _row_gather_start(src_hbm, idx_ref, buf, sem, slot, n_rows):
    def body(g, carry):
        _row_gather_groups(src_hbm, idx_ref, buf, sem, slot, g, 1)
        return carry

    lax.fori_loop(0, n_rows // F32_SUBLANES, body, 0, unroll=2)


def _row_gather_wait(buf, sem, slot):
    pltpu.make_async_copy(buf.at[slot], buf.at[slot], sem.at[slot]).wait()


def _moe_expert_kernel(te_ref, na_ref, src_cur_ref, src_nxt_ref, x_hbm, g_ref, wg_ref, wu_ref, wd_ref, o_ref,
                       buf, xg_ref, acc_ref, sem, *, tm, nf):
    i = pl.program_id(0)
    f = pl.program_id(1)
    n_active = na_ref[0]
    active = i < n_active
    slot = i % 2
    d = xg_ref.shape[1]
    groups_per_step = tm // F32_SUBLANES // nf

    @pl.when((f == 0) & (i == 0))
    def _():
        _row_gather_start(x_hbm, src_cur_ref, buf, sem, 0, tm)

    @pl.when((f == 0) & (i <= n_active))
    def _():
        _row_gather_wait(buf, sem, slot)

    @pl.when((f == 0) & active)
    def _():
        xg_ref[...] = _rms(buf[slot].reshape(tm, d), g_ref[...]).astype(BF16)
        acc_ref[...] = jnp.zeros_like(acc_ref)

    @pl.when(active)
    def _():
        _row_gather_groups(x_hbm, src_nxt_ref, buf, sem, 1 - slot, f * groups_per_step, groups_per_step)
        xn = xg_ref[...]
        hg = _mm(xn, wg_ref[0])
        hu = _mm(xn, wu_ref[0])
        a = (hg * _sigmoid(hg) * hu).astype(BF16)
        acc_ref[...] += _mm(a, wd_ref[0])

    last = f == pl.num_programs(1) - 1

    @pl.when(last & active)
    def _():
        o_ref[...] = acc_ref[...]

    @pl.when(last & jnp.logical_not(active))
    def _():
        o_ref[...] = jnp.zeros_like(o_ref)


def _moe_combine_kernel(pos_cur_ref, pos_nxt_ref, x_ref, w_ref, y_hbm, o_ref, buf, sem, *, tc):
    i = pl.program_id(0)
    slot = i % 2

    @pl.when(i == 0)
    def _():
        _row_gather_start(y_hbm, pos_cur_ref, buf, sem, 0, 2 * tc)

    _row_gather_wait(buf, sem, slot)

    @pl.when(i + 1 < pl.num_programs(0))
    def _():
        _row_gather_start(y_hbm, pos_nxt_ref, buf, sem, 1 - slot, 2 * tc)

    w = w_ref[...]
    d = x_ref.shape[1]
    half = tc // F32_SUBLANES
    y0 = buf[slot, 0:half].reshape(tc, d)
    y1 = buf[slot, half:2 * half].reshape(tc, d)
    o_ref[...] = x_ref[...] + w[:, 0:1] * y0 + w[:, 1:2] * y1


MOE_ROW_TILE = 1024
MOE_COMBINE_TILE = 512


def _moe_plan(e_idx, tm, tc):
    n = e_idx.shape[0]
    n_assign = 2 * n
    max_tiles = n_assign // tm + N_EXPERTS
    e_flat = e_idx.reshape(n_assign)
    onehot = (e_flat[:, None] == jnp.arange(N_EXPERTS, dtype=jnp.int32)[None, :]).astype(jnp.int32)
    csum = jnp.cumsum(onehot, axis=0)
    rank = jnp.sum(onehot * csum, axis=1) - 1
    counts = csum[-1]
    padded = ((counts + tm - 1) // tm) * tm
    gend = jnp.cumsum(padded)
    pos = (gend - padded)[e_flat] + rank
    n_active = (gend[-1] // tm).astype(jnp.int32)
    tile_start = jnp.arange(max_tiles, dtype=jnp.int32) * tm
    tile_expert = jnp.sum((tile_start[:, None] >= gend[None, :]).astype(jnp.int32), axis=1)
    last_expert = jnp.sum((tile_start[jnp.maximum(n_active - 1, 0)] >= gend).astype(jnp.int32))
    tile_expert = jnp.where(tile_start < gend[-1], tile_expert, last_expert).astype(jnp.int32)
    src = jnp.zeros((max_tiles * tm,), jnp.int32).at[pos].set(jnp.arange(n_assign, dtype=jnp.int32) // 2)
    pos_tiles = pos.reshape(n // tc, tc, 2).transpose(0, 2, 1).reshape(n_assign)
    return src, tile_expert, n_active.reshape(1), pos_tiles.astype(jnp.int32), max_tiles


def _moe(x, g, w_router, w_gu, w_d):
    n, d = x.shape
    ff = w_d.shape[1]
    tm = min(MOE_ROW_TILE, n)
    tc = min(MOE_COMBINE_TILE, n // 2)
    tf = _pick(ff, (896, 512, 256, 128))
    nf = ff // tf
    assert (2 * n) % tm == 0 and tm % (F32_SUBLANES * nf) == 0
    idx, wts = _router(x, g, w_router)
    src, tile_expert, n_active, pos_tiles, max_tiles = _moe_plan(idx[:, :2], tm, tc)

    def wmap(col_off):
        def f(i, f_, te, na):
            return (te[i], 0, jnp.where(i < na[0], f_, nf - 1) + col_off)
        return f

    smem = functools.partial(pl.BlockSpec, memory_space=pltpu.SMEM)
    y = pl.pallas_call(
        functools.partial(_moe_expert_kernel, tm=tm, nf=nf),
        grid_spec=pltpu.PrefetchScalarGridSpec(
            num_scalar_prefetch=2,
            grid=(max_tiles, nf),
            in_specs=[smem((tm,), lambda i, f_, te, na: (i,)),
                      smem((tm,), lambda i, f_, te, na: (jnp.minimum(i + 1, max_tiles - 1),)),
                      pl.BlockSpec(memory_space=pl.ANY),
                      pl.BlockSpec((1, d), lambda i, f_, te, na: (0, 0)),
                      pl.BlockSpec((1, d, tf), wmap(0)),
                      pl.BlockSpec((1, d, tf), wmap(nf)),
                      pl.BlockSpec((1, tf, d), lambda i, f_, te, na: (te[i], jnp.where(i < na[0], f_, nf - 1), 0))],
            out_specs=pl.BlockSpec((tm, d), lambda i, f_, te, na: (i, 0)),
            scratch_shapes=[pltpu.VMEM((2, tm // F32_SUBLANES, F32_SUBLANES, d), F32), pltpu.VMEM((tm, d), BF16),
                            pltpu.VMEM((tm, d), F32), pltpu.SemaphoreType.DMA((2,))]),
        out_shape=jax.ShapeDtypeStruct((max_tiles * tm, d), F32),
        compiler_params=_cp("arbitrary", "arbitrary"),
        name="moe_experts",
    )(tile_expert, n_active, src, src, x, g.reshape(1, d), w_gu, w_gu, w_d)

    n_steps = n // tc
    return pl.pallas_call(
        functools.partial(_moe_combine_kernel, tc=tc),
        grid=(n_steps,),
        in_specs=[smem((2 * tc,), lambda i: (i,)),
                  smem((2 * tc,), lambda i: (jnp.minimum(i + 1, n_steps - 1),)),
                  pl.BlockSpec((tc, d), lambda i: (i, 0)),
                  pl.BlockSpec((tc, 128), lambda i: (i, 0)),
                  pl.BlockSpec(memory_space=pl.ANY)],
        out_specs=pl.BlockSpec((tc, d), lambda i: (i, 0)),
        out_shape=jax.ShapeDtypeStruct((n, d), F32),
        scratch_shapes=[pltpu.VMEM((2, 2 * tc // F32_SUBLANES, F32_SUBLANES, d), F32), pltpu.SemaphoreType.DMA((2,))],
        compiler_params=_cp("arbitrary"),
        name="moe_combine",
    )(pos_tiles, pos_tiles, x, wts, y)


NA_GROUP = 4
NA_KEYS = NA_WIN_ROWS * GRID_W
NA_BLK = 8 * GRID_W


def _na_bias_table(rpb):
    c = jnp.arange(GRID_W)
    cs = jnp.clip(c - NA_WIN_COLS // 2, 0, GRID_W - NA_WIN_COLS)
    cp = jnp.arange(GRID_W)
    valid = (cp[None, :] >= cs[:, None]) & (cp[None, :] < cs[:, None] + NA_WIN_COLS)
    colrel = jnp.clip(cp[None, :] - c[:, None] + NA_WIN_COLS - 1, 0, 2 * NA_WIN_COLS - 2)
    rr = jnp.arange(NA_WIN_ROWS)[:, None] + jnp.arange(NA_WIN_ROWS)[None, :]
    tbl = rpb.astype(F32)[:, rr][:, :, :, colrel]
    tbl = jnp.where(valid[None, None, None], tbl, NEG_BIG)
    tbl = tbl.reshape(NA_HEADS // NA_GROUP, NA_GROUP, NA_WIN_ROWS, NA_WIN_ROWS, GRID_W, GRID_W)
    tbl = tbl.transpose(2, 0, 3, 5, 1, 4)
    return tbl.reshape(NA_WIN_ROWS, NA_HEADS // NA_GROUP, NA_KEYS, NA_GROUP * GRID_W)


def _na_kernel(q_ref, kp_ref, kc_ref, kn_ref, vp_ref, vc_ref, vn_ref, bias_ref, o_ref, kst, vst, *, rows):
    r = pl.program_id(0) % rows
    i = r // 8

    @pl.when(r % 8 == 0)
    def _():
        kst[0:NA_BLK] = kp_ref[...]
        kst[NA_BLK:2 * NA_BLK] = kc_ref[...]
        kst[2 * NA_BLK:3 * NA_BLK] = kn_ref[...]
        vst[0:NA_BLK] = vp_ref[...]
        vst[NA_BLK:2 * NA_BLK] = vc_ref[...]
        vst[2 * NA_BLK:3 * NA_BLK] = vn_ref[...]

    rs = jnp.clip(r - NA_WIN_ROWS // 2, 0, rows - NA_WIN_ROWS)
    off = pl.multiple_of((rs - 8 * (i - 1)) * GRID_W, GRID_W)
    gw = NA_GROUP * NA_HEAD_DIM
    lane_head = lax.broadcasted_iota(jnp.int32, (GRID_W, gw), 1) // NA_HEAD_DIM
    for grp in range(NA_HEADS // NA_GROUP):
        sl = slice(grp * gw, (grp + 1) * gw)
        qg = q_ref[:, sl] * (NA_HEAD_DIM ** -0.5)
        bq = jnp.concatenate([jnp.where(lane_head == h, qg, jnp.zeros_like(qg)) for h in range(NA_GROUP)], axis=0)
        kw = kst[pl.ds(off, NA_KEYS), sl]
        st = _mm_nt(kw, bq) + bias_ref[0, grp]
        p = jnp.exp(st - jnp.max(st, axis=0, keepdims=True))
        p = (p / jnp.sum(p, axis=0, keepdims=True)).astype(BF16)
        vw = vst[pl.ds(off, NA_KEYS), sl]
        res = _mm_tn(p, vw)
        o = jnp.zeros((GRID_W, gw), F32)
        for h in range(NA_GROUP):
            o = o + jnp.where(lane_head == h, res[h * GRID_W:(h + 1) * GRID_W], 0.0)
        o_ref[:, sl] = o.astype(o_ref.dtype)


def _na_attention(qkv, batch, seq_len, bias_tbl):
    n = qkv.shape[0]
    d = D_MODEL
    rows = seq_len // GRID_W
    assert rows % 8 == 0 and rows >= 16
    nb = rows // 8

    def kmap(delta, col):
        def f(g):
            b = g // rows
            i = (g % rows) // 8
            return (b * nb + jnp.clip(i + delta, 0, nb - 1), col)
        return f

    def bias_map(g):
        r = g % rows
        return (jnp.clip(r - NA_WIN_ROWS // 2, 0, rows - NA_WIN_ROWS) - r + NA_WIN_ROWS - 1, 0, 0, 0)

    kv_specs = [pl.BlockSpec((NA_BLK, d), kmap(dl, col)) for col in (1, 2) for dl in (-1, 0, 1)]
    return pl.pallas_call(
        functools.partial(_na_kernel, rows=rows),
        grid=(batch * rows,),
        in_specs=[pl.BlockSpec((GRID_W, d), lambda g: (g, 0))] + kv_specs
                 + [pl.BlockSpec((1,) + bias_tbl.shape[1:], bias_map)],
        out_specs=pl.BlockSpec((GRID_W, d), lambda g: (g, 0)),
        out_shape=jax.ShapeDtypeStruct((n, d), BF16),
        scratch_shapes=[pltpu.VMEM((3 * NA_BLK, d), BF16), pltpu.VMEM((3 * NA_BLK, d), BF16)],
        compiler_params=_cp("arbitrary"),
        name="na_attention",
    )(qkv, qkv, qkv, qkv, qkv, qkv, qkv, bias_tbl)


def _na_mixer(x, batch, seq_len, g, w_qkv, rpb, w_out):
    qkv = _norm_matmul(x, g, w_qkv)
    o = _na_attention(qkv, batch, seq_len, _na_bias_table(rpb))
    return _proj_res(lambda a: a, [(o, D_MODEL, 0, None)], [], w_out, x, "na_out")


LRU_LANES = 128


def _halo_maps(seq_len, tile, halo, col, tile_of):
    per_tile = tile // halo
    per_seq = seq_len // halo

    def prev(*g):
        return (jnp.maximum(g[0] * per_seq + tile_of(*g) * per_tile - 1, g[0] * per_seq), col)

    def nxt(*g):
        return (jnp.minimum(g[0] * per_seq + (tile_of(*g) + 1) * per_tile, (g[0] + 1) * per_seq - 1), col)

    return prev, nxt


def _lru_kernel(cur_ref, prev_ref, next_ref, cw_ref, cb_ref, wg_ref, ba_ref, bx_ref, lam_ref, o_ref,
                a_scr, b_scr, h_scr, carry, *, tile, n_tiles):
    d = pl.program_id(1)
    i = pl.program_id(2)
    ti = jnp.where(d == 0, i, n_tiles - 1 - i)
    hl = BF16_SUBLANES
    prev = jnp.where(ti == 0, 0.0, prev_ref[...].astype(F32))
    nxt = jnp.where(ti == n_tiles - 1, 0.0, next_ref[...].astype(F32))
    xp = jnp.concatenate([prev, cur_ref[...].astype(F32), nxt], axis=0)
    left = LRU_CONV // 2
    xc = cb_ref[...]
    for k in range(LRU_CONV):
        s = hl + k - left
        xc = xc + xp[s:s + tile] * cw_ref[k:k + 1, :]
    xcb = xc.astype(BF16)
    ga, gx = [], []
    for grp in range(D_MODEL // LRU_LANES):
        gg = _mm(xcb[:, grp * LRU_LANES:(grp + 1) * LRU_LANES], wg_ref[0, grp])
        ga.append(gg[:, :LRU_LANES])
        gx.append(gg[:, LRU_LANES:])
    ga = jnp.concatenate(ga, axis=-1) + ba_ref[0]
    gx = jnp.concatenate(gx, axis=-1) + bx_ref[0]
    log_a = -LRU_C * _sigmoid(ga) * _softplus(-lam_ref[0])
    a = jnp.exp(log_a)
    a_scr[...] = a
    b_scr[...] = jnp.sqrt(1.0 - a * a) * (_sigmoid(gx) * xc)

    @pl.when(i == 0)
    def _():
        carry[...] = jnp.zeros_like(carry)

    def body(t, h):
        tt = jnp.where(d == 0, t, tile - 1 - t)
        h = a_scr[pl.ds(tt, 1), :] * h + b_scr[pl.ds(tt, 1), :]
        h_scr[pl.ds(tt, 1), :] = h
        return h

    carry[...] = lax.fori_loop(0, tile, body, carry[...], unroll=8)
    o_ref[0] = h_scr[...].astype(o_ref.dtype)


def _lru_gate_weights(ga_w, gx_w):
    def bd(w):
        w = w.reshape(2, D_MODEL // LRU_LANES, 2, 64, 64)
        z = jnp.zeros_like(w[:, :, 0])
        top = jnp.concatenate([w[:, :, 0], z], axis=-1)
        bot = jnp.concatenate([z, w[:, :, 1]], axis=-1)
        return jnp.concatenate([top, bot], axis=-2)
    return jnp.concatenate([bd(ga_w), bd(gx_w)], axis=-1).astype(BF16)


def _lru_scan(gb, batch, seq_len, conv_w, conv_b, wg, ba, bx, lam):
    n = gb.shape[0]
    w = D_MODEL
    tile = _pick(seq_len, (512, 256))
    n_tiles = seq_len // tile
    hl = BF16_SUBLANES

    def tile_of(b, d, i):
        return jnp.where(d == 0, i, n_tiles - 1 - i)

    prev_map, next_map = _halo_maps(seq_len, tile, hl, 1, tile_of)
    row = lambda b, d, i: (d, 0, 0)
    return pl.pallas_call(
        functools.partial(_lru_kernel, tile=tile, n_tiles=n_tiles),
        grid=(batch, 2, n_tiles),
        in_specs=[pl.BlockSpec((tile, w), lambda b, d, i: (b * n_tiles + tile_of(b, d, i), 1)),
                  pl.BlockSpec((hl, w), prev_map),
                  pl.BlockSpec((hl, w), next_map),
                  pl.BlockSpec((LRU_CONV, w), lambda b, d, i: (0, 0)),
                  pl.BlockSpec((1, w), lambda b, d, i: (0, 0)),
                  pl.BlockSpec((1,) + wg.shape[1:], lambda b, d, i: (d, 0, 0, 0)),
                  pl.BlockSpec((1, 1, w), row), pl.BlockSpec((1, 1, w), row), pl.BlockSpec((1, 1, w), row)],
        out_specs=pl.BlockSpec((1, tile, w), lambda b, d, i: (d, b * n_tiles + tile_of(b, d, i), 0)),
        out_shape=jax.ShapeDtypeStruct((2, n, w), BF16),
        scratch_shapes=[pltpu.VMEM((tile, w), F32), pltpu.VMEM((tile, w), F32), pltpu.VMEM((tile, w), F32),
                        pltpu.VMEM((1, w), F32)],
        compiler_params=_cp("parallel", "arbitrary", "arbitrary"),
        name="lru_scan",
    )(gb, gb, gb, conv_w, conv_b.reshape(1, w), wg, ba.reshape(2, 1, w), bx.reshape(2, 1, w), lam.reshape(2, 1, w))


def _gelu_tanh(x):
    return 0.5 * x * (1.0 + jnp.tanh(math.sqrt(2.0 / math.pi) * (x + 0.044715 * (x * x * x))))


def _lru_mixer(x, batch, seq_len, g, w_in, conv_w, conv_b, ga_w, ga_b, gx_w, gx_b, lam, w_out):
    gb = _norm_matmul(x, g, w_in)
    h = _lru_scan(gb, batch, seq_len, conv_w, conv_b, _lru_gate_weights(ga_w, gx_w), ga_b, gx_b, lam)

    def prologue(gate, hf, hb):
        return _gelu_tanh(gate.astype(F32)) * (hf.astype(F32) + hb.astype(F32))

    return _proj_res(prologue, [(gb, D_MODEL, 0, None), (h, D_MODEL, 0, 0), (h, D_MODEL, 0, 1)], [], w_out, x,
                     "lru_out")


GLA_EXTRA = 128


def _gla_kernel(q_ref, k_ref, v_ref, lr_ref, w2_ref, gb_ref, o_ref, qb_scr, kb_scr, kd_scr, dec_scr, st_scr,
                *, tile, n_tiles):
    d = pl.program_id(1)
    i = pl.program_id(2)
    c = GLA_CHUNK
    n_chunks = tile // c

    @pl.when(i == 0)
    def _():
        st_scr[...] = jnp.zeros_like(st_scr)

    z = _mm(lr_ref[...], w2_ref[0]) + gb_ref[0]
    g = -_softplus(-z) / GLA_TAU
    row = lax.broadcasted_iota(jnp.int32, (tile, tile), 0)
    col = lax.broadcasted_iota(jnp.int32, (tile, tile), 1)
    same = (row // c) == (col // c)
    fwd = d == 0
    tri = same & (jnp.where(fwd, col, row) <= jnp.where(fwd, row, col))
    tri = jnp.where(tri, 1.0, 0.0).astype(BF16)
    ones = jnp.where(same, 1.0, 0.0).astype(BF16)
    g_hi = g.astype(BF16)
    g_lo = (g - g_hi.astype(F32)).astype(BF16)
    bsum = _mm(tri, g_hi) + _mm(tri, g_lo)
    tot = _mm(ones, g_hi) + _mm(ones, g_lo)
    q = q_ref[...].astype(F32) * (GLA_HK ** -0.5)
    k = k_ref[...].astype(F32)
    qb_scr[...] = (q * jnp.exp(bsum)).astype(BF16)
    kb_scr[...] = (k * jnp.exp(-bsum)).astype(BF16)
    kd_scr[...] = (k * jnp.exp(tot - bsum)).astype(BF16)
    dec_scr[...] = jnp.exp(tot)

    def chunk_pass(forward):
        keep = same & ((col <= row) if forward else (col > row))
        order = range(n_chunks) if forward else range(n_chunks - 1, -1, -1)
        for h in range(GLA_HEADS):
            ks = slice(h * GLA_HK, (h + 1) * GLA_HK)
            vs = slice(h * GLA_HV, (h + 1) * GLA_HV)
            qb = qb_scr[:, ks]
            vh = v_ref[:, vs]
            att = jnp.where(keep, _mm_nt(qb, kb_scr[:, ks]), 0.0).astype(BF16)
            o_intra = _mm(att, vh)
            s_t = st_scr[h]
            for ci in order:
                rows = slice(ci * c, (ci + 1) * c)
                o = o_intra[rows] + _mm_nt(qb[rows], s_t.astype(BF16))
                o_ref[0, rows, vs] = o.astype(o_ref.dtype)
                s_t = s_t * dec_scr[ci * c:ci * c + 1, ks] + _mm_tn(vh[rows], kd_scr[rows, ks])
            st_scr[h] = s_t

    @pl.when(fwd)
    def _():
        chunk_pass(True)

    @pl.when(jnp.logical_not(fwd))
    def _():
        chunk_pass(False)


def _gla_scan(proj, batch, seq_len, w2p, gate_b):
    n = proj.shape[0]
    tile = _pick(seq_len, (256,))
    n_tiles = seq_len // tile

    def tok(b, d, i):
        return b * n_tiles + jnp.where(d == 0, i, n_tiles - 1 - i)

    return pl.pallas_call(
        functools.partial(_gla_kernel, tile=tile, n_tiles=n_tiles),
        grid=(batch, 2, n_tiles),
        in_specs=[pl.BlockSpec((tile, GLA_DK), lambda b, d, i: (tok(b, d, i), 0)),
                  pl.BlockSpec((tile, GLA_DK), lambda b, d, i: (tok(b, d, i), 1)),
                  pl.BlockSpec((tile, GLA_DV), lambda b, d, i: (tok(b, d, i), 1)),
                  pl.BlockSpec((tile, GLA_EXTRA), lambda b, d, i: (tok(b, d, i), (2 * GLA_DK + 2 * GLA_DV) // GLA_EXTRA)),
                  pl.BlockSpec((1, GLA_EXTRA, GLA_DK), lambda b, d, i: (d, 0, 0)),
                  pl.BlockSpec((1, 1, GLA_DK), lambda b, d, i: (d, 0, 0))],
        out_specs=pl.BlockSpec((1, tile, GLA_DV), lambda b, d, i: (d, tok(b, d, i), 0)),
        out_shape=jax.ShapeDtypeStruct((2, n, GLA_DV), BF16),
        scratch_shapes=[pltpu.VMEM((tile, GLA_DK), BF16), pltpu.VMEM((tile, GLA_DK), BF16),
                        pltpu.VMEM((tile, GLA_DK), BF16), pltpu.VMEM((tile, GLA_DK), F32),
                        pltpu.VMEM((GLA_HEADS, GLA_HV, GLA_HK), F32)],
        compiler_params=_cp("parallel", "arbitrary", "arbitrary"),
        name="gla_scan",
    )(proj, proj, proj, proj, w2p, gate_b.reshape(2, 1, GLA_DK))


def _gla_mixer(x, batch, seq_len, g, w_in, gate_w1, gate_w2, gate_b, head_norm, w_out):
    d = D_MODEL
    pad = jnp.zeros((d, GLA_EXTRA - 2 * GLA_GATE_RANK), w_in.dtype)
    w_ext = jnp.concatenate([w_in, gate_w1[0].astype(w_in.dtype), gate_w1[1].astype(w_in.dtype), pad], axis=1)
    proj = _norm_matmul(x, g, w_ext)
    w2p = jnp.zeros((2, GLA_EXTRA, GLA_DK), F32)
    w2p = w2p.at[0, :GLA_GATE_RANK].set(gate_w2[0]).at[1, GLA_GATE_RANK:2 * GLA_GATE_RANK].set(gate_w2[1])
    o = _gla_scan(proj, batch, seq_len, w2p.astype(BF16), gate_b)

    def prologue(of, ob, r, hn):
        o_sum = of.astype(F32) + ob.astype(F32)
        parts = []
        for h in range(GLA_HEADS):
            oh = o_sum[:, h * GLA_HV:(h + 1) * GLA_HV]
            parts.append(_rms(oh, hn))
        r = r.astype(F32)
        return jnp.concatenate(parts, axis=-1) * (r * _sigmoid(r))

    return _proj_res(prologue, [(o, GLA_DV, 0, 0), (o, GLA_DV, 0, 1), (proj, GLA_DV, 2, None)],
                     [head_norm.reshape(1, GLA_HV).astype(F32)], w_out, x, "gla_out")


HY_PAD = 128


def _hy_filter_kernel(bands_ref, w1t_ref, w1c_ref, w1s_ref, b1_ref, w2_ref, b2_ref, w3_ref, freq_ref, ld_ref,
                      h_ref, s_ref, *, seq_len, tile):
    i = pl.program_id(0)
    hp = lax.Precision.HIGHEST
    j = (i * tile + lax.broadcasted_iota(jnp.int32, (tile, 1), 0)).astype(F32)
    t = j / seq_len
    ang = (2.0 * math.pi / seq_len) * j * bands_ref[...]
    pre = (t * w1t_ref[...]
           + jnp.dot(jnp.cos(ang), w1c_ref[...], precision=hp, preferred_element_type=F32)
           + jnp.dot(jnp.sin(-ang), w1s_ref[...], precision=hp, preferred_element_type=F32)
           + b1_ref[...])
    h = jnp.sin(freq_ref[0:1, :] * pre)
    h = jnp.sin(freq_ref[1:2, :] * (jnp.dot(h, w2_ref[...], precision=hp, preferred_element_type=F32) + b2_ref[...]))
    h = jnp.dot(h, w3_ref[...], precision=hp, preferred_element_type=F32)
    dist = jnp.abs(j - (seq_len // 2)) / (seq_len / 2)
    h = h * jnp.exp(-dist * jnp.exp(ld_ref[...]))
    h_ref[...] = h.astype(h_ref.dtype)

    @pl.when(i == 0)
    def _():
        s_ref[...] = jnp.zeros_like(s_ref)

    s_ref[...] += jnp.sum(jnp.abs(h), axis=0, keepdims=True)


def _hy_filter(seq_len, w1, b1, w2, b2, w3, freq, log_decay):
    hid = HY_FILTER_HID
    c2 = w3.shape[1]
    tile = _pick(seq_len, (512, 256))
    bands = jnp.zeros((1, HY_PAD), F32).at[0, :HY_BANDS].set(jnp.linspace(1e-4, HY_BANDS - 1, HY_BANDS, dtype=F32))
    w1 = w1.astype(F32)
    w1c = jnp.zeros((HY_PAD, hid), F32).at[:HY_BANDS].set(w1[1:1 + HY_BANDS])
    w1s = jnp.zeros((HY_PAD, hid), F32).at[:HY_BANDS].set(w1[1 + HY_BANDS:])
    full = lambda shape: pl.BlockSpec(shape, lambda i: (0, 0))
    return pl.pallas_call(
        functools.partial(_hy_filter_kernel, seq_len=seq_len, tile=tile),
        grid=(seq_len // tile,),
        in_specs=[full((1, HY_PAD)), full((1, hid)), full((HY_PAD, hid)), full((HY_PAD, hid)), full((1, hid)),
                  full((hid, hid)), full((1, hid)), full((hid, c2)), full((2, hid)), full((1, c2))],
        out_specs=[pl.BlockSpec((tile, c2), lambda i: (i, 0)), pl.BlockSpec((1, c2), lambda i: (0, 0))],
        out_shape=[jax.ShapeDtypeStruct((seq_len, c2), BF16), jax.ShapeDtypeStruct((1, c2), F32)],
        compiler_params=_cp("arbitrary"),
        name="hy_filter",
    )(bands, w1[0:1], w1c, w1s, b1.reshape(1, hid).astype(F32), w2.astype(F32), b2.reshape(1, hid).astype(F32),
      w3.astype(F32), freq.astype(F32), log_decay.reshape(1, c2).astype(F32))


def _hy_pre_kernel(cur_ref, prev_ref, next_ref, w_ref, b_ref, o_ref, *, tile, n_tiles):
    i = pl.program_id(1)
    hl = BF16_SUBLANES
    prev = jnp.where(i == 0, 0.0, prev_ref[...].astype(F32))
    nxt = jnp.where(i == n_tiles - 1, 0.0, next_ref[...].astype(F32))
    xp = jnp.concatenate([prev, cur_ref[...].astype(F32), nxt], axis=0)
    y = b_ref[...]
    for k in range(3):
        s = hl + k - 1
        y = y + xp[s:s + tile] * w_ref[k:k + 1, :]
    o_ref[...] = y.astype(o_ref.dtype)


def _hy_pre(u, batch, seq_len, short_w, short_b):
    n, c3 = u.shape
    w = D_MODEL
    tile = _pick(seq_len, (512, 256))
    n_tiles = seq_len // tile
    hl = BF16_SUBLANES
    per_tile, per_seq = tile // hl, seq_len // hl
    return pl.pallas_call(
        functools.partial(_hy_pre_kernel, tile=tile, n_tiles=n_tiles),
        grid=(batch, n_tiles, c3 // w),
        in_specs=[pl.BlockSpec((tile, w), lambda b, i, c: (b * n_tiles + i, c)),
                  pl.BlockSpec((hl, w), lambda b, i, c: (jnp.maximum(b * per_seq + i * per_tile - 1, b * per_seq), c)),
                  pl.BlockSpec((hl, w), lambda b, i, c: (jnp.minimum(b * per_seq + (i + 1) * per_tile,
                                                                     (b + 1) * per_seq - 1), c)),
                  pl.BlockSpec((3, w), lambda b, i, c: (0, c)),
                  pl.BlockSpec((1, w), lambda b, i, c: (0, c))],
        out_specs=pl.BlockSpec((tile, w), lambda b, i, c: (b * n_tiles + i, c)),
        out_shape=jax.ShapeDtypeStruct((n, c3), BF16),
        compiler_params=_cp("parallel", "parallel", "parallel"),
        name="hy_pre",
    )(u, u, u, short_w.astype(F32), short_b.reshape(1, c3).astype(F32))


def _fft_plan(seq_len):
    n = 2 * seq_len
    n2 = 256 if n >= 32768 else 128
    n1 = n // n2
    assert n1 % 4 == 0 and n1 * n2 == n
    return n, n1, n2, n1 // 2


def _fft_tables(seq_len):
    n, n1, n2, ah = _fft_plan(seq_len)
    b = jnp.arange(n2, dtype=jnp.int32)
    ka = jnp.arange(n1, dtype=jnp.int32)
    a = jnp.arange(ah, dtype=jnp.int32)
    w = 2.0 * math.pi / n
    m = (ka[None, :, None] * (b[:, None, None] + n2 * a[None, None, :])) % n
    ang = w * m.astype(F32)
    ft = jnp.concatenate([jnp.cos(ang), -jnp.sin(ang)], axis=1).astype(BF16)
    ap = a + n1 // 4
    m = ((b[:, None, None] + n2 * ap[None, :, None]) * ka[None, None, :]) % n
    ang = w * m.astype(F32)
    fti = (jnp.concatenate([jnp.cos(ang), -jnp.sin(ang)], axis=2) / n).astype(BF16)
    ang = (2.0 * math.pi / n2) * ((b[:, None] * b[None, :]) % n2).astype(F32)
    fr, fi = jnp.cos(ang), -jnp.sin(ang)
    m2 = jnp.block([[fr, -fi], [fi, fr]]).astype(BF16)
    m2i = jnp.block([[fr, fi], [-fi, fr]]).astype(BF16)
    return ft, fti, m2, m2i


def _fft_first_kernel(z_ref, ft_ref, o_ref, *, tb):
    for i in range(tb):
        o_ref[0, i] = _mm(ft_ref[i], z_ref[0, i]).astype(o_ref.dtype)


def _fft_first(zb, ft):
    bsz, n2, ah, c = zb.shape
    rows = ft.shape[1]
    cb = 1024
    tb = _pick(n2, (16, 8))
    return pl.pallas_call(
        functools.partial(_fft_first_kernel, tb=tb),
        grid=(bsz, n2 // tb, c // cb),
        in_specs=[pl.BlockSpec((1, tb, ah, cb), lambda b, j, cc: (b, j, 0, cc)),
                  pl.BlockSpec((tb, rows, ah), lambda b, j, cc: (j, 0, 0))],
        out_specs=pl.BlockSpec((1, tb, rows, cb), lambda b, j, cc: (b, j, 0, cc)),
        out_shape=jax.ShapeDtypeStruct((bsz, n2, rows, c), BF16),
        compiler_params=_cp("parallel", "parallel", "parallel"),
        name="fft_first",
    )(zb, ft)


def _fft_spectrum_kernel(a_ref, s_ref, m_ref, o_ref, *, tk):
    for i in range(tk):
        o_ref[0, i] = (_mm(m_ref[...], a_ref[0, i]) / s_ref[...]).astype(o_ref.dtype)


def _fft_conv_kernel(a_ref, h_ref, m_ref, mi_ref, o_ref, *, tk, n2):
    for i in range(tk):
        x = _mm(m_ref[...], a_ref[0, i])
        xr, xi = x[:n2], x[n2:]
        hr, hi = h_ref[0, i, :n2].astype(F32), h_ref[0, i, n2:].astype(F32)
        p = jnp.concatenate([xr * hr - xi * hi, xr * hi + xi * hr], axis=0).astype(BF16)
        o_ref[0, i] = _mm(mi_ref[...], p).astype(o_ref.dtype)


def _fft_mid(ak, m2, m2i=None, spec=None, spec_col=0, scale=None):
    bsz, n1, r2, c = ak.shape
    cb = 1024
    tk = _pick(n1, (4, 2))
    grid = (bsz, n1 // tk, c // cb)
    a_spec = pl.BlockSpec((1, tk, r2, cb), lambda b, k, cc: (b, k, 0, cc))
    m_spec = pl.BlockSpec((r2, r2), lambda b, k, cc: (0, 0))
    if spec is None:
        return pl.pallas_call(
            functools.partial(_fft_spectrum_kernel, tk=tk),
            grid=grid,
            in_specs=[a_spec, pl.BlockSpec((1, cb), lambda b, k, cc: (0, cc)), m_spec],
            out_specs=a_spec,
            out_shape=jax.ShapeDtypeStruct(ak.shape, BF16),
            compiler_params=_cp("parallel", "parallel", "parallel"),
            name="fft_spectrum",
        )(ak, scale, m2)
    return pl.pallas_call(
        functools.partial(_fft_conv_kernel, tk=tk, n2=r2 // 2),
        grid=grid,
        in_specs=[a_spec, pl.BlockSpec((1, tk, r2, cb), lambda b, k, cc: (0, k, 0, spec_col + cc)), m_spec, m_spec],
        out_specs=a_spec,
        out_shape=jax.ShapeDtypeStruct(ak.shape, BF16),
        compiler_params=_cp("parallel", "parallel", "parallel"),
        name="fft_conv",
    )(ak, spec, m2, m2i)


def _fft_last_kernel(c_ref, fti_ref, o_ref, *, tb):
    for i in range(tb):
        o_ref[0, i] = _mm(fti_ref[i], c_ref[0, i]).astype(o_ref.dtype)


def _fft_last(cb_arr, fti):
    bsz, n2, rows, c = cb_arr.shape
    ah = fti.shape[1]
    cb = 1024
    tb = _pick(n2, (16, 8))
    return pl.pallas_call(
        functools.partial(_fft_last_kernel, tb=tb),
        grid=(bsz, n2 // tb, c // cb),
        in_specs=[pl.BlockSpec((1, tb, rows, cb), lambda b, j, cc: (b, j, 0, cc)),
                  pl.BlockSpec((tb, ah, rows), lambda b, j, cc: (j, 0, 0))],
        out_specs=pl.BlockSpec((1, tb, ah, cb), lambda b, j, cc: (b, j, 0, cc)),
        out_shape=jax.ShapeDtypeStruct((bsz, n2, ah, c), BF16),
        compiler_params=_cp("parallel", "parallel", "parallel"),
        name="fft_last",
    )(cb_arr, fti)


def _to_b_major(z, bsz, n2, ah):
    c = z.shape[-1]
    return z.reshape(bsz, ah, n2, c).transpose(0, 2, 1, 3)


def _swap_digits(arr, inner):
    bsz, p, r, c = arr.shape
    return arr.reshape(bsz, p, 2, inner, c).transpose(0, 3, 2, 1, 4).reshape(bsz, inner, 2 * p, c)


def _fft_forward(z, bsz, seq_len, ft):
    _, n1, n2, ah = _fft_plan(seq_len)
    a = _fft_first(_to_b_major(z, bsz, n2, ah), ft)
    return _swap_digits(a, n1)


def _long_conv(z, bsz, seq_len, tables, spec, order):
    ft, fti, m2, m2i = tables
    _, n1, n2, ah = _fft_plan(seq_len)
    ak = _fft_forward(z, bsz, seq_len, ft)
    ck = _fft_mid(ak, m2, m2i, spec=spec, spec_col=order * (D_MODEL // 1024))
    y = _fft_last(_swap_digits(ck, n2), fti)
    return y.transpose(0, 2, 1, 3).reshape(bsz * seq_len, z.shape[-1])


def _hy_gate_kernel(x1_ref, y_ref, v_ref, skip_ref, o_ref):
    v = v_ref[...].astype(F32)
    o_ref[...] = (x1_ref[...].astype(F32) * (y_ref[...].astype(F32) + skip_ref[...] * v)).astype(o_ref.dtype)


def _hy_gate(u3, y, skip0):
    n, w = y.shape
    tm = _pick(n, (1024, 512, 256))
    return pl.pallas_call(
        _hy_gate_kernel,
        grid=(n // tm,),
        in_specs=[pl.BlockSpec((tm, w), lambda i: (i, 1)), pl.BlockSpec((tm, w), lambda i: (i, 0)),
                  pl.BlockSpec((tm, w), lambda i: (i, 0)), pl.BlockSpec((1, w), lambda i: (0, 0))],
        out_specs=pl.BlockSpec((tm, w), lambda i: (i, 0)),
        out_shape=jax.ShapeDtypeStruct((n, w), BF16),
        compiler_params=_cp("parallel"),
        name="hy_gate",
    )(u3, y, u3, skip0.reshape(1, w).astype(F32))


def _hyena_mixer(x, batch, seq_len, g, w_in, short_w, short_b, fw1, fb1, fw2, fb2, fw3, ffreq, log_decay, skip,
                 w_out):
    d = D_MODEL
    tables = _fft_tables(seq_len)
    filt, l1 = _hy_filter(seq_len, fw1, fb1, fw2, fb2, fw3, ffreq, log_decay)
    fk = _fft_forward(filt, 1, seq_len, tables[0])
    spec = _fft_mid(fk, tables[2], scale=l1)
    u3 = _hy_pre(_norm_matmul(x, g, w_in), batch, seq_len, short_w, short_b)
    y1 = _long_conv(u3[:, :d], batch, seq_len, tables, spec, 0)
    z1 = _hy_gate(u3, y1, skip[0])
    y2 = _long_conv(z1, batch, seq_len, tables, spec, 1)

    def prologue(x2, yy, zz, sk):
        return x2.astype(F32) * (yy.astype(F32) + sk * zz.astype(F32))

    return _proj_res(prologue, [(u3, d, 2, None), (y2, d, 0, None), (z1, d, 0, None)],
                     [skip[1].reshape(1, d).astype(F32)], w_out, x, "hy_out")


def _trunk(x, mem, p):
    batch, seq_len, d = x.shape
    x = x.reshape(batch * seq_len, d)
    mem = mem.reshape(batch * MEM_TOKENS, d)
    depth = p["norm_mix"].shape[0]
    for i in range(depth):
        m, j = i % 4, i // 4
        g = p["norm_mix"][i]
        if m == 0:
            x = _na_mixer(x, batch, seq_len, g, p["na_w_qkv"][j], p["na_rpb"][j], p["na_w_out"][j])
        elif m == 1:
            x = _lru_mixer(x, batch, seq_len, g, p["lru_w_in"][j], p["lru_conv_w"][j], p["lru_conv_b"][j],
                           p["lru_gate_a_w"][j], p["lru_gate_a_b"][j], p["lru_gate_x_w"][j], p["lru_gate_x_b"][j],
                           p["lru_lambda"][j], p["lru_w_out"][j])
        elif m == 2:
            x = _gla_mixer(x, batch, seq_len, g, p["gla_w_in"][j], p["gla_gate_w1"][j], p["gla_gate_w2"][j],
                           p["gla_gate_b"][j], p["gla_head_norm"][j], p["gla_w_out"][j])
        else:
            x = _hyena_mixer(x, batch, seq_len, g, p["hy_w_in"][j], p["hy_short_w"][j], p["hy_short_b"][j],
                             p["hy_filt_w1"][j], p["hy_filt_b1"][j], p["hy_filt_w2"][j], p["hy_filt_b2"][j],
                             p["hy_filt_w3"][j], p["hy_filt_freq"][j], p["hy_log_decay"][j], p["hy_skip"][j],
                             p["hy_w_out"][j])
        kv = _norm_matmul(mem, p["norm_mem"][i], p["xa_w_kv"][i])
        x = _xattn(x, seq_len, p["norm_xattn"][i], p["xa_w_q"][i], kv, p["xa_w_out"][i])
        if i % 2 == 0:
            x = _ffn(x, p["norm_ffn"][i], p["ffn_w_gate_up"][i // 2], p["ffn_w_down"][i // 2])
        else:
            x = _moe(x, p["norm_ffn"][i], p["moe_router"][i // 2], p["moe_w_gate_up"][i // 2], p["moe_w_down"][i // 2])
    return _final_norm(x, p["norm_final"]).reshape(batch, seq_len, d)


_MATMUL_WEIGHTS = ("na_w_qkv", "na_w_out", "lru_w_in", "lru_w_out", "gla_w_in", "gla_w_out", "hy_w_in", "hy_w_out",
                   "xa_w_q", "xa_w_kv", "xa_w_out", "ffn_w_gate_up", "ffn_w_down", "moe_w_gate_up", "moe_w_down")


def kernel(x_prompt, x_sample, mem_prompt, mem_sample, norm_mix, norm_xattn, norm_mem, norm_ffn, norm_final, na_w_qkv, na_rpb, na_w_out, lru_w_in, lru_conv_w, lru_conv_b, lru_gate_a_w, lru_gate_a_b, lru_gate_x_w, lru_gate_x_b, lru_lambda, lru_w_out, gla_w_in, gla_gate_w1, gla_gate_w2, gla_gate_b, gla_head_norm, gla_w_out, hy_w_in, hy_short_w, hy_short_b, hy_filt_w1, hy_filt_b1, hy_filt_w2, hy_filt_b2, hy_filt_w3, hy_filt_freq, hy_log_decay, hy_skip, hy_w_out, xa_w_q, xa_w_kv, xa_w_out, ffn_w_gate_up, ffn_w_down, moe_router, moe_w_gate_up, moe_w_down):
    p = dict(norm_mix=norm_mix, norm_xattn=norm_xattn, norm_mem=norm_mem, norm_ffn=norm_ffn, norm_final=norm_final,
             na_w_qkv=na_w_qkv, na_rpb=na_rpb, na_w_out=na_w_out,
             lru_w_in=lru_w_in, lru_conv_w=lru_conv_w, lru_conv_b=lru_conv_b, lru_gate_a_w=lru_gate_a_w,
             lru_gate_a_b=lru_gate_a_b, lru_gate_x_w=lru_gate_x_w, lru_gate_x_b=lru_gate_x_b, lru_lambda=lru_lambda,
             lru_w_out=lru_w_out,
             gla_w_in=gla_w_in, gla_gate_w1=gla_gate_w1, gla_gate_w2=gla_gate_w2, gla_gate_b=gla_gate_b,
             gla_head_norm=gla_head_norm, gla_w_out=gla_w_out,
             hy_w_in=hy_w_in, hy_short_w=hy_short_w, hy_short_b=hy_short_b, hy_filt_w1=hy_filt_w1,
             hy_filt_b1=hy_filt_b1, hy_filt_w2=hy_filt_w2, hy_filt_b2=hy_filt_b2, hy_filt_w3=hy_filt_w3,
             hy_filt_freq=hy_filt_freq, hy_log_decay=hy_log_decay, hy_skip=hy_skip, hy_w_out=hy_w_out,
             xa_w_q=xa_w_q, xa_w_kv=xa_w_kv, xa_w_out=xa_w_out,
             ffn_w_gate_up=ffn_w_gate_up, ffn_w_down=ffn_w_down,
             moe_router=moe_router, moe_w_gate_up=moe_w_gate_up, moe_w_down=moe_w_down)
    for name in _MATMUL_WEIGHTS:
        p[name] = p[name].astype(BF16)
    return (_trunk(x_prompt, mem_prompt, p), _trunk(x_sample, mem_sample, p))
```

```python
import functools
import math

import jax
import jax.numpy as jnp
from jax import lax
from jax.experimental import pallas as pl
from jax.experimental.pallas import tpu as pltpu

F32 = jnp.float32
BF16 = jnp.bfloat16

D_MODEL = 1024
RMS_EPS = 1e-6
GRID_W = 64
NA_HEADS = 16
NA_HEAD_DIM = 64
NA_WIN_ROWS = 8
NA_WIN_COLS = 16
LRU_C = 8.0
LRU_CONV = 4
GLA_HEADS = 4
GLA_DK = 512
GLA_DV = 1024
GLA_HK = 128
GLA_HV = 256
GLA_GATE_RANK = 16
GLA_TAU = 16.0
GLA_CHUNK = 64
HY_BANDS = 16
HY_FILTER_HID = 64
MEM_TOKENS = 256
XA_HEADS = 4
XA_HEAD_DIM = 256
D_FF = 3584
N_EXPERTS = 8
NEG_BIG = -1e30

V7X_VMEM_BYTES = 64 * 1024 * 1024
VMEM_LIMIT = V7X_VMEM_BYTES - 8 * 1024 * 1024
BF16_SUBLANES = 16


def _cp(*sem):
    return pltpu.CompilerParams(dimension_semantics=sem, vmem_limit_bytes=VMEM_LIMIT)


def _pick(n, cands):
    for c in cands:
        if n % c == 0:
            return c
    return n


def _rms(x, g):
    return x * lax.rsqrt(jnp.mean(x * x, axis=-1, keepdims=True) + RMS_EPS) * g


def _sigmoid(x):
    return 1.0 / (1.0 + jnp.exp(-x))


def _softplus(x):
    return jnp.maximum(x, 0.0) + jnp.log1p(jnp.exp(-jnp.abs(x)))


def _mm(a, b):
    return jnp.dot(a, b, preferred_element_type=F32)


def _mm_nt(a, b):
    return lax.dot_general(a, b, (((1,), (1,)), ((), ())), preferred_element_type=F32)


def _mm_tn(a, b):
    return lax.dot_general(a, b, (((0,), (0,)), ((), ())), preferred_element_type=F32)


def _norm_matmul_kernel(x_ref, g_ref, w_ref, o_ref, xn_ref):
    @pl.when(pl.program_id(1) == 0)
    def _():
        xn_ref[...] = _rms(x_ref[...], g_ref[...]).astype(BF16)

    o_ref[...] = _mm(xn_ref[...], w_ref[...]).astype(o_ref.dtype)


def _norm_matmul(x, g, w, out_dtype=BF16):
    n, d = x.shape
    f = w.shape[1]
    tm = _pick(n, (1024, 512, 256))
    tn = _pick(f, (1024, 640, 512, 256, 128))
    return pl.pallas_call(
        _norm_matmul_kernel,
        grid=(n // tm, f // tn),
        in_specs=[pl.BlockSpec((tm, d), lambda i, j: (i, 0)),
                  pl.BlockSpec((1, d), lambda i, j: (0, 0)),
                  pl.BlockSpec((d, tn), lambda i, j: (0, j))],
        out_specs=pl.BlockSpec((tm, tn), lambda i, j: (i, j)),
        out_shape=jax.ShapeDtypeStruct((n, f), out_dtype),
        scratch_shapes=[pltpu.VMEM((tm, d), BF16)],
        compiler_params=_cp("parallel", "arbitrary"),
        name="norm_matmul",
    )(x, g.reshape(1, d), w)


def _final_norm_kernel(x_ref, g_ref, o_ref):
    o_ref[...] = _rms(x_ref[...], g_ref[...])


def _final_norm(x, g):
    n, d = x.shape
    tm = _pick(n, (1024, 512, 256))
    return pl.pallas_call(
        _final_norm_kernel,
        grid=(n // tm,),
        in_specs=[pl.BlockSpec((tm, d), lambda i: (i, 0)), pl.BlockSpec((1, d), lambda i: (0, 0))],
        out_specs=pl.BlockSpec((tm, d), lambda i: (i, 0)),
        out_shape=jax.ShapeDtypeStruct((n, d), F32),
        compiler_params=_cp("parallel"),
        name="final_norm",
    )(x, g.reshape(1, d))


def _proj_res(prologue, tok_inputs, row_inputs, w, res, name):
    n, d = res.shape
    k = w.shape[0]
    tm = _pick(n, (512, 256))
    n_tok, n_row = len(tok_inputs), len(row_inputs)

    def kern(*refs):
        toks = []
        for r, (_, _, _, lead) in zip(refs[:n_tok], tok_inputs):
            toks.append(r[0] if lead is not None else r[...])
        rows = [r[...] for r in refs[n_tok:n_tok + n_row]]
        w_ref, res_ref, o_ref = refs[n_tok + n_row:]
        a = prologue(*toks, *rows)
        o_ref[...] = res_ref[...] + _mm(a.astype(BF16), w_ref[...])

    in_specs, args = [], []
    for arr, width, col, lead in tok_inputs:
        if lead is None:
            in_specs.append(pl.BlockSpec((tm, width), lambda i, c=col: (i, c)))
        else:
            in_specs.append(pl.BlockSpec((1, tm, width), lambda i, c=col, l=lead: (l, i, c)))
        args.append(arr)
    for arr in row_inputs:
        in_specs.append(pl.BlockSpec(arr.shape, lambda i: (0, 0)))
        args.append(arr)
    in_specs += [pl.BlockSpec((k, d), lambda i: (0, 0)), pl.BlockSpec((tm, d), lambda i: (i, 0))]
    args += [w, res]
    return pl.pallas_call(
        kern,
        grid=(n // tm,),
        in_specs=in_specs,
        out_specs=pl.BlockSpec((tm, d), lambda i: (i, 0)),
        out_shape=jax.ShapeDtypeStruct((n, d), F32),
        compiler_params=_cp("parallel"),
        name=name,
    )(*args)


def _xattn_kernel(x_ref, g_ref, wq_ref, k_ref, v_ref, wo_ref, o_ref):
    x = x_ref[...]
    xn = _rms(x, g_ref[...]).astype(BF16)
    q = (_mm(xn, wq_ref[...]) * (XA_HEAD_DIM ** -0.5)).astype(BF16)
    outs = []
    for h in range(XA_HEADS):
        sl = slice(h * XA_HEAD_DIM, (h + 1) * XA_HEAD_DIM)
        s = _mm_nt(q[:, sl], k_ref[:, sl])
        p = jnp.exp(s - jnp.max(s, axis=-1, keepdims=True))
        p = p / jnp.sum(p, axis=-1, keepdims=True)
        outs.append(_mm(p.astype(BF16), v_ref[:, sl]))
    o = jnp.concatenate(outs, axis=-1).astype(BF16)
    o_ref[...] = x + _mm(o, wo_ref[...])


def _xattn(x, seq_len, g, wq, kv, wo):
    n, d = x.shape
    tm = _pick(seq_len, (512, 256))
    per_b = seq_len // tm
    m = MEM_TOKENS
    return pl.pallas_call(
        _xattn_kernel,
        grid=(n // tm,),
        in_specs=[pl.BlockSpec((tm, d), lambda i: (i, 0)),
                  pl.BlockSpec((1, d), lambda i: (0, 0)),
                  pl.BlockSpec((d, d), lambda i: (0, 0)),
                  pl.BlockSpec((m, d), lambda i: (i // per_b, 0)),
                  pl.BlockSpec((m, d), lambda i: (i // per_b, 1)),
                  pl.BlockSpec((d, d), lambda i: (0, 0))],
        out_specs=pl.BlockSpec((tm, d), lambda i: (i, 0)),
        out_shape=jax.ShapeDtypeStruct((n, d), F32),
        compiler_params=_cp("parallel"),
        name="xattn",
    )(x, g.reshape(1, d), wq, kv, kv, wo)


def _ffn_kernel(x_ref, g_ref, wg_ref, wu_ref, wd_ref, o_ref, xn_ref, acc_ref):
    f = pl.program_id(1)

    @pl.when(f == 0)
    def _():
        xn_ref[...] = _rms(x_ref[...], g_ref[...]).astype(BF16)
        acc_ref[...] = jnp.zeros_like(acc_ref)

    xn = xn_ref[...]
    hg = _mm(xn, wg_ref[...])
    hu = _mm(xn, wu_ref[...])
    a = (hg * _sigmoid(hg) * hu).astype(BF16)
    acc_ref[...] += _mm(a, wd_ref[...])

    @pl.when(f == pl.num_programs(1) - 1)
    def _():
        o_ref[...] = x_ref[...] + acc_ref[...]


def _ffn(x, g, w_gu, w_d):
    n, d = x.shape
    ff = w_d.shape[0]
    tm = _pick(n, (1024, 512, 256))
    tf = _pick(ff, (512, 256, 128))
    nf = ff // tf
    return pl.pallas_call(
        _ffn_kernel,
        grid=(n // tm, nf),
        in_specs=[pl.BlockSpec((tm, d), lambda i, f: (i, 0)),
                  pl.BlockSpec((1, d), lambda i, f: (0, 0)),
                  pl.BlockSpec((d, tf), lambda i, f: (0, f)),
                  pl.BlockSpec((d, tf), lambda i, f: (0, f + nf)),
                  pl.BlockSpec((tf, d), lambda i, f: (f, 0))],
        out_specs=pl.BlockSpec((tm, d), lambda i, f: (i, 0)),
        out_shape=jax.ShapeDtypeStruct((n, d), F32),
        scratch_shapes=[pltpu.VMEM((tm, d), BF16), pltpu.VMEM((tm, d), F32)],
        compiler_params=_cp("parallel", "arbitrary"),
        name="ffn",
    )(x, g.reshape(1, d), w_gu, w_gu, w_d)


def _router_kernel(x_ref, g_ref, wh_ref, wl_ref, idx_ref, wts_ref):
    xn = _rms(x_ref[...], g_ref[...])
    xh = xn.astype(BF16)
    xl = (xn - xh.astype(F32)).astype(BF16)
    logits = _mm(xh, wh_ref[...]) + _mm(xh, wl_ref[...]) + _mm(xl, wh_ref[...])
    lane = lax.broadcasted_iota(jnp.int32, logits.shape, 1)
    logits = jnp.where(lane < N_EXPERTS, logits, NEG_BIG)
    m1 = jnp.max(logits, axis=-1, keepdims=True)
    i1 = jnp.min(jnp.where(logits == m1, lane, 128), axis=-1, keepdims=True)
    rest = jnp.where(lane == i1, NEG_BIG, logits)
    m2 = jnp.max(rest, axis=-1, keepdims=True)
    i2 = jnp.min(jnp.where(rest == m2, lane, 128), axis=-1, keepdims=True)
    e2 = jnp.exp(m2 - m1)
    den = 1.0 + e2
    idx_ref[...] = jnp.where(lane == 0, i1, jnp.where(lane == 1, i2, 0))
    wts_ref[...] = jnp.where(lane == 0, 1.0 / den, jnp.where(lane == 1, e2 / den, 0.0))


def _router(x, g, w_router):
    n, d = x.shape
    tm = _pick(n, (1024, 512, 256))
    wp = jnp.zeros((d, 128), F32).at[:, :N_EXPERTS].set(w_router)
    wh = wp.astype(BF16)
    wl = (wp - wh.astype(F32)).astype(BF16)
    return pl.pallas_call(
        _router_kernel,
        grid=(n // tm,),
        in_specs=[pl.BlockSpec((tm, d), lambda i: (i, 0)),
                  pl.BlockSpec((1, d), lambda i: (0, 0)),
                  pl.BlockSpec((d, 128), lambda i: (0, 0)),
                  pl.BlockSpec((d, 128), lambda i: (0, 0))],
        out_specs=[pl.BlockSpec((tm, 128), lambda i: (i, 0)), pl.BlockSpec((tm, 128), lambda i: (i, 0))],
        out_shape=[jax.ShapeDtypeStruct((n, 128), jnp.int32), jax.ShapeDtypeStruct((n, 128), F32)],
        compiler_params=_cp("parallel"),
        name="router",
    )(x, g.reshape(1, d), wh, wl)


F32_SUBLANES = 8


def _row_copy(src_hbm, row, buf, sem, slot, grp, sub):
    return pltpu.make_async_copy(src_hbm.at[pl.ds(row, 1), :], buf.at[slot, grp, pl.ds(sub, 1), :], sem.at[slot])


def _row_gather_groups(src_hbm, idx_ref, buf, sem, slot, first_grp, n_groups):
    for g in range(n_groups):
        for sub in range(F32_SUBLANES):
            row = idx_ref[(first_grp + g) * F32_SUBLANES + sub]
            _row_copy(src_hbm, row, buf, sem, slot, first_grp + g, sub).start()


def _row_gather_start(src_hbm, idx_ref, buf, sem, slot, n_rows):
    def body(g, carry):
        _row_gather_groups(src_hbm, idx_ref, buf, sem, slot, g, 1)
        return carry

    lax.fori_loop(0, n_rows // F32_SUBLANES, body, 0, unroll=2)


def _row_gather_wait(buf, sem, slot):
    pltpu.make_async_copy(buf.at[slot], buf.at[slot], sem.at[slot]).wait()


def _moe_expert_kernel(te_ref, na_ref, src_cur_ref, src_nxt_ref, x_hbm, g_ref, wg_ref, wu_ref, wd_ref, o_ref,
                       buf, xg_ref, acc_ref, sem, *, tm, nf):
    i = pl.program_id(0)
    f = pl.program_id(1)
    n_active = na_ref[0]
    active = i < n_active
    slot = i % 2
    d = xg_ref.shape[1]
    n_groups = tm // F32_SUBLANES
    groups_main = -(-n_groups // nf)
    groups_last = n_groups - groups_main * (nf - 1)
    assert nf >= 3 and groups_last > 0

    @pl.when((f == 0) & (i == 0))
    def _():
        _row_gather_start(x_hbm, src_cur_ref, buf, sem, 0, tm)

    @pl.when((f == 0) & (i <= n_active))
    def _():
        _row_gather_wait(buf, sem, slot)

    @pl.when((f == 0) & active)
    def _():
        xg_ref[...] = _rms(buf[slot].reshape(tm, d), g_ref[...]).astype(BF16)

    def step(n_issue, sink):
        _row_gather_groups(x_hbm, src_nxt_ref, buf, sem, 1 - slot, f * groups_main, n_issue)
        xn = xg_ref[...]
        hg = _mm(xn, wg_ref[0])
        hu = _mm(xn, wu_ref[0])
        a = (hg * _sigmoid(hg) * hu).astype(BF16)
        sink(_mm(a, wd_ref[0]))

    def set_acc(y):
        acc_ref[...] = y

    def add_acc(y):
        acc_ref[...] += y

    def emit(y):
        o_ref[...] = acc_ref[...] + y

    pl.when(active & (f == 0))(lambda: step(groups_main, set_acc))
    pl.when(active & (f > 0) & (f < nf - 1))(lambda: step(groups_main, add_acc))
    pl.when(active & (f == nf - 1))(lambda: step(groups_last, emit))

    @pl.when((f == nf - 1) & jnp.logical_not(active))
    def _():
        o_ref[...] = jnp.zeros_like(o_ref)


def _moe_combine_kernel(pos_cur_ref, pos_nxt_ref, x_ref, w_ref, y_hbm, o_ref, buf, sem, *, tc):
    i = pl.program_id(0)
    slot = i % 2

    @pl.when(i == 0)
    def _():
        _row_gather_start(y_hbm, pos_cur_ref, buf, sem, 0, 2 * tc)

    _row_gather_wait(buf, sem, slot)

    @pl.when(i + 1 < pl.num_programs(0))
    def _():
        _row_gather_start(y_hbm, pos_nxt_ref, buf, sem, 1 - slot, 2 * tc)

    w = w_ref[...]
    d = x_ref.shape[1]
    half = tc // F32_SUBLANES
    y0 = buf[slot, 0:half].reshape(tc, d)
    y1 = buf[slot, half:2 * half].reshape(tc, d)
    o_ref[...] = x_ref[...] + w[:, 0:1] * y0 + w[:, 1:2] * y1


MOE_ROW_TILE = 1024
MOE_COMBINE_TILE = 512


def _moe_plan(e_idx, tm, tc):
    n = e_idx.shape[0]
    n_assign = 2 * n
    max_tiles = n_assign // tm + N_EXPERTS
    e_flat = e_idx.reshape(n_assign)
    onehot = (e_flat[:, None] == jnp.arange(N_EXPERTS, dtype=jnp.int32)[None, :]).astype(jnp.int32)
    csum = jnp.cumsum(onehot, axis=0)
    rank = jnp.sum(onehot * csum, axis=1) - 1
    counts = csum[-1]
    padded = ((counts + tm - 1) // tm) * tm
    gend = jnp.cumsum(padded)
    pos = (gend - padded)[e_flat] + rank
    n_active = (gend[-1] // tm).astype(jnp.int32)
    tile_start = jnp.arange(max_tiles, dtype=jnp.int32) * tm
    tile_expert = jnp.sum((tile_start[:, None] >= gend[None, :]).astype(jnp.int32), axis=1)
    last_expert = jnp.sum((tile_start[jnp.maximum(n_active - 1, 0)] >= gend).astype(jnp.int32))
    tile_expert = jnp.where(tile_start < gend[-1], tile_expert, last_expert).astype(jnp.int32)
    src = jnp.zeros((max_tiles * tm,), jnp.int32).at[pos].set(
        jnp.arange(n_assign, dtype=jnp.int32) // 2, unique_indices=True, mode="promise_in_bounds")
    pos_tiles = pos.reshape(n // tc, tc, 2).transpose(0, 2, 1).reshape(n_assign)
    return src, tile_expert, n_active.reshape(1), pos_tiles.astype(jnp.int32), max_tiles


def _moe(x, g, w_router, w_gu, w_d):
    n, d = x.shape
    ff = w_d.shape[1]
    tm = min(MOE_ROW_TILE, n)
    tc = min(MOE_COMBINE_TILE, n // 2)
    tf = _pick(ff, (512, 256, 128))
    nf = ff // tf
    assert (2 * n) % tm == 0 and tm % F32_SUBLANES == 0
    idx, wts = _router(x, g, w_router)
    src, tile_expert, n_active, pos_tiles, max_tiles = _moe_plan(idx[:, :2], tm, tc)

    def wmap(col_off):
        def f(i, f_, te, na):
            return (te[i], 0, jnp.where(i < na[0], f_, nf - 1) + col_off)
        return f

    smem = functools.partial(pl.BlockSpec, memory_space=pltpu.SMEM)
    y = pl.pallas_call(
        functools.partial(_moe_expert_kernel, tm=tm, nf=nf),
        grid_spec=pltpu.PrefetchScalarGridSpec(
            num_scalar_prefetch=2,
            grid=(max_tiles, nf),
            in_specs=[smem((tm,), lambda i, f_, te, na: (i,)),
                      smem((tm,), lambda i, f_, te, na: (jnp.minimum(i + 1, max_tiles - 1),)),
                      pl.BlockSpec(memory_space=pl.ANY),
                      pl.BlockSpec((1, d), lambda i, f_, te, na: (0, 0)),
                      pl.BlockSpec((1, d, tf), wmap(0)),
                      pl.BlockSpec((1, d, tf), wmap(nf)),
                      pl.BlockSpec((1, tf, d), lambda i, f_, te, na: (te[i], jnp.where(i < na[0], f_, nf - 1), 0))],
            out_specs=pl.BlockSpec((tm, d), lambda i, f_, te, na: (i, 0)),
            scratch_shapes=[pltpu.VMEM((2, tm // F32_SUBLANES, F32_SUBLANES, d), F32), pltpu.VMEM((tm, d), BF16),
                            pltpu.VMEM((tm, d), F32), pltpu.SemaphoreType.DMA((2,))]),
        out_shape=jax.ShapeDtypeStruct((max_tiles * tm, d), F32),
        compiler_params=_cp("arbitrary", "arbitrary"),
        name="moe_experts",
    )(tile_expert, n_active, src, src, x, g.reshape(1, d), w_gu, w_gu, w_d)

    n_steps = n // tc
    return pl.pallas_call(
        functools.partial(_moe_combine_kernel, tc=tc),
        grid=(n_steps,),
        in_specs=[smem((2 * tc,), lambda i: (i,)),
                  smem((2 * tc,), lambda i: (jnp.minimum(i + 1, n_steps - 1),)),
                  pl.BlockSpec((tc, d), lambda i: (i, 0)),
                  pl.BlockSpec((tc, 128), lambda i: (i, 0)),
                  pl.BlockSpec(memory_space=pl.ANY)],
        out_specs=pl.BlockSpec((tc, d), lambda i: (i, 0)),
        out_shape=jax.ShapeDtypeStruct((n, d), F32),
        scratch_shapes=[pltpu.VMEM((2, 2 * tc // F32_SUBLANES, F32_SUBLANES, d), F32), pltpu.SemaphoreType.DMA((2,))],
        compiler_params=_cp("arbitrary"),
        name="moe_combine",
    )(pos_tiles, pos_tiles, x, wts, y)


NA_GROUP = 4
NA_KEYS = NA_WIN_ROWS * GRID_W
NA_BLK = 8 * GRID_W


def _na_bias_table(rpb):
    c = jnp.arange(GRID_W)
    cs = jnp.clip(c - NA_WIN_COLS // 2, 0, GRID_W - NA_WIN_COLS)
    cp = jnp.arange(GRID_W)
    valid = (cp[None, :] >= cs[:, None]) & (cp[None, :] < cs[:, None] + NA_WIN_COLS)
    colrel = jnp.clip(cp[None, :] - c[:, None] + NA_WIN_COLS - 1, 0, 2 * NA_WIN_COLS - 2)
    rr = jnp.arange(NA_WIN_ROWS)[:, None] + jnp.arange(NA_WIN_ROWS)[None, :]
    tbl = rpb.astype(F32)[:, rr][:, :, :, colrel]
    tbl = jnp.where(valid[None, None, None], tbl, NEG_BIG)
    tbl = tbl.reshape(NA_HEADS // NA_GROUP, NA_GROUP, NA_WIN_ROWS, NA_WIN_ROWS, GRID_W, GRID_W)
    tbl = tbl.transpose(2, 0, 3, 5, 1, 4)
    return tbl.reshape(NA_WIN_ROWS, NA_HEADS // NA_GROUP, NA_KEYS, NA_GROUP * GRID_W)


def _na_kernel(q_ref, kp_ref, kc_ref, kn_ref, vp_ref, vc_ref, vn_ref, bias_ref, o_ref, kst, vst, *, rows):
    r = pl.program_id(0) % rows
    i = r // 8

    @pl.when(r % 8 == 0)
    def _():
        kst[0:NA_BLK] = kp_ref[...]
        kst[NA_BLK:2 * NA_BLK] = kc_ref[...]
        kst[2 * NA_BLK:3 * NA_BLK] = kn_ref[...]
        vst[0:NA_BLK] = vp_ref[...]
        vst[NA_BLK:2 * NA_BLK] = vc_ref[...]
        vst[2 * NA_BLK:3 * NA_BLK] = vn_ref[...]

    rs = jnp.clip(r - NA_WIN_ROWS // 2, 0, rows - NA_WIN_ROWS)
    off = pl.multiple_of((rs - 8 * (i - 1)) * GRID_W, GRID_W)
    gw = NA_GROUP * NA_HEAD_DIM
    lane_head = lax.broadcasted_iota(jnp.int32, (GRID_W, gw), 1) // NA_HEAD_DIM
    for grp in range(NA_HEADS // NA_GROUP):
        sl = slice(grp * gw, (grp + 1) * gw)
        qg = q_ref[:, sl] * (NA_HEAD_DIM ** -0.5)
        bq = jnp.concatenate([jnp.where(lane_head == h, qg, jnp.zeros_like(qg)) for h in range(NA_GROUP)], axis=0)
        kw = kst[pl.ds(off, NA_KEYS), sl]
        st = _mm_nt(kw, bq) + bias_ref[0, grp]
        p = jnp.exp(st - jnp.max(st, axis=0, keepdims=True))
        p = (p / jnp.sum(p, axis=0, keepdims=True)).astype(BF16)
        vw = vst[pl.ds(off, NA_KEYS), sl]
        res = _mm_tn(p, vw)
        o = jnp.zeros((GRID_W, gw), F32)
        for h in range(NA_GROUP):
            o = o + jnp.where(lane_head == h, res[h * GRID_W:(h + 1) * GRID_W], 0.0)
        o_ref[:, sl] = o.astype(o_ref.dtype)


def _na_attention(qkv, batch, seq_len, bias_tbl):
    n = qkv.shape[0]
    d = D_MODEL
    rows = seq_len // GRID_W
    assert rows % 8 == 0 and rows >= 16
    nb = rows // 8

    def kmap(delta, col):
        def f(g):
            b = g // rows
            i = (g % rows) // 8
            return (b * nb + jnp.clip(i + delta, 0, nb - 1), col)
        return f

    def bias_map(g):
        r = g % rows
        return (jnp.clip(r - NA_WIN_ROWS // 2, 0, rows - NA_WIN_ROWS) - r + NA_WIN_ROWS - 1, 0, 0, 0)

    kv_specs = [pl.BlockSpec((NA_BLK, d), kmap(dl, col)) for col in (1, 2) for dl in (-1, 0, 1)]
    return pl.pallas_call(
        functools.partial(_na_kernel, rows=rows),
        grid=(batch * rows,),
        in_specs=[pl.BlockSpec((GRID_W, d), lambda g: (g, 0))] + kv_specs
                 + [pl.BlockSpec((1,) + bias_tbl.shape[1:], bias_map)],
        out_specs=pl.BlockSpec((GRID_W, d), lambda g: (g, 0)),
        out_shape=jax.ShapeDtypeStruct((n, d), BF16),
        scratch_shapes=[pltpu.VMEM((3 * NA_BLK, d), BF16), pltpu.VMEM((3 * NA_BLK, d), BF16)],
        compiler_params=_cp("arbitrary"),
        name="na_attention",
    )(qkv, qkv, qkv, qkv, qkv, qkv, qkv, bias_tbl)


def _na_mixer(x, batch, seq_len, g, w_qkv, rpb, w_out):
    qkv = _norm_matmul(x, g, w_qkv)
    o = _na_attention(qkv, batch, seq_len, _na_bias_table(rpb))
    return _proj_res(lambda a: a, [(o, D_MODEL, 0, None)], [], w_out, x, "na_out")


LRU_LANES = 128


def _halo_maps(seq_len, tile, halo, col, tile_of):
    per_tile = tile // halo
    per_seq = seq_len // halo

    def prev(*g):
        return (jnp.maximum(g[0] * per_seq + tile_of(*g) * per_tile - 1, g[0] * per_seq), col)

    def nxt(*g):
        return (jnp.minimum(g[0] * per_seq + (tile_of(*g) + 1) * per_tile, (g[0] + 1) * per_seq - 1), col)

    return prev, nxt


def _lru_kernel(cur_ref, prev_ref, next_ref, cw_ref, cb_ref, wg_ref, ba_ref, bx_ref, lam_ref, o_ref,
                a_scr, b_scr, h_scr, carry, *, tile, n_tiles):
    d = pl.program_id(1)
    i = pl.program_id(2)
    ti = jnp.where(d == 0, i, n_tiles - 1 - i)
    hl = BF16_SUBLANES
    prev = jnp.where(ti == 0, 0.0, prev_ref[...].astype(F32))
    nxt = jnp.where(ti == n_tiles - 1, 0.0, next_ref[...].astype(F32))
    xp = jnp.concatenate([prev, cur_ref[...].astype(F32), nxt], axis=0)
    left = LRU_CONV // 2
    xc = cb_ref[...]
    for k in range(LRU_CONV):
        s = hl + k - left
        xc = xc + xp[s:s + tile] * cw_ref[k:k + 1, :]
    xcb = xc.astype(BF16)
    ga, gx = [], []
    for grp in range(D_MODEL // LRU_LANES):
        gg = _mm(xcb[:, grp * LRU_LANES:(grp + 1) * LRU_LANES], wg_ref[0, grp])
        ga.append(gg[:, :LRU_LANES])
        gx.append(gg[:, LRU_LANES:])
    ga = jnp.concatenate(ga, axis=-1) + ba_ref[0]
    gx = jnp.concatenate(gx, axis=-1) + bx_ref[0]
    log_a = -LRU_C * _sigmoid(ga) * _softplus(-lam_ref[0])
    a = jnp.exp(log_a)
    a_scr[...] = a
    b_scr[...] = jnp.sqrt(1.0 - a * a) * (_sigmoid(gx) * xc)

    @pl.when(i == 0)
    def _():
        carry[...] = jnp.zeros_like(carry)

    def body(t, h):
        tt = jnp.where(d == 0, t, tile - 1 - t)
        h = a_scr[pl.ds(tt, 1), :] * h + b_scr[pl.ds(tt, 1), :]
        h_scr[pl.ds(tt, 1), :] = h
        return h

    carry[...] = lax.fori_loop(0, tile, body, carry[...], unroll=8)
    o_ref[0] = h_scr[...].astype(o_ref.dtype)


def _lru_gate_weights(ga_w, gx_w):
    def bd(w):
        w = w.reshape(2, D_MODEL // LRU_LANES, 2, 64, 64)
        z = jnp.zeros_like(w[:, :, 0])
        top = jnp.concatenate([w[:, :, 0], z], axis=-1)
        bot = jnp.concatenate([z, w[:, :, 1]], axis=-1)
        return jnp.concatenate([top, bot], axis=-2)
    return jnp.concatenate([bd(ga_w), bd(gx_w)], axis=-1).astype(BF16)


def _lru_scan(gb, batch, seq_len, conv_w, conv_b, wg, ba, bx, lam):
    n = gb.shape[0]
    w = D_MODEL
    tile = _pick(seq_len, (512, 256))
    n_tiles = seq_len // tile
    hl = BF16_SUBLANES

    def tile_of(b, d, i):
        return jnp.where(d == 0, i, n_tiles - 1 - i)

    prev_map, next_map = _halo_maps(seq_len, tile, hl, 1, tile_of)
    row = lambda b, d, i: (d, 0, 0)
    return pl.pallas_call(
        functools.partial(_lru_kernel, tile=tile, n_tiles=n_tiles),
        grid=(batch, 2, n_tiles),
        in_specs=[pl.BlockSpec((tile, w), lambda b, d, i: (b * n_tiles + tile_of(b, d, i), 1)),
                  pl.BlockSpec((hl, w), prev_map),
                  pl.BlockSpec((hl, w), next_map),
                  pl.BlockSpec((LRU_CONV, w), lambda b, d, i: (0, 0)),
                  pl.BlockSpec((1, w), lambda b, d, i: (0, 0)),
                  pl.BlockSpec((1,) + wg.shape[1:], lambda b, d, i: (d, 0, 0, 0)),
                  pl.BlockSpec((1, 1, w), row), pl.BlockSpec((1, 1, w), row), pl.BlockSpec((1, 1, w), row)],
        out_specs=pl.BlockSpec((1, tile, w), lambda b, d, i: (d, b * n_tiles + tile_of(b, d, i), 0)),
        out_shape=jax.ShapeDtypeStruct((2, n, w), BF16),
        scratch_shapes=[pltpu.VMEM((tile, w), F32), pltpu.VMEM((tile, w), F32), pltpu.VMEM((tile, w), F32),
                        pltpu.VMEM((1, w), F32)],
        compiler_params=_cp("parallel", "arbitrary", "arbitrary"),
        name="lru_scan",
    )(gb, gb, gb, conv_w, conv_b.reshape(1, w), wg, ba.reshape(2, 1, w), bx.reshape(2, 1, w), lam.reshape(2, 1, w))


def _gelu_tanh(x):
    return 0.5 * x * (1.0 + jnp.tanh(math.sqrt(2.0 / math.pi) * (x + 0.044715 * (x * x * x))))


def _lru_mixer(x, batch, seq_len, g, w_in, conv_w, conv_b, ga_w, ga_b, gx_w, gx_b, lam, w_out):
    gb = _norm_matmul(x, g, w_in)
    h = _lru_scan(gb, batch, seq_len, conv_w, conv_b, _lru_gate_weights(ga_w, gx_w), ga_b, gx_b, lam)

    def prologue(gate, hf, hb):
        return _gelu_tanh(gate.astype(F32)) * (hf.astype(F32) + hb.astype(F32))

    return _proj_res(prologue, [(gb, D_MODEL, 0, None), (h, D_MODEL, 0, 0), (h, D_MODEL, 0, 1)], [], w_out, x,
                     "lru_out")


GLA_EXTRA = 128


def _gla_kernel(q_ref, k_ref, v_ref, lr_ref, w2_ref, gb_ref, o_ref, qb_scr, kb_scr, kd_scr, dec_scr, st_scr,
                *, tile, n_tiles):
    d = pl.program_id(1)
    i = pl.program_id(2)
    c = GLA_CHUNK
    n_chunks = tile // c

    @pl.when(i == 0)
    def _():
        st_scr[...] = jnp.zeros_like(st_scr)

    z = _mm(lr_ref[...], w2_ref[0]) + gb_ref[0]
    g = -_softplus(-z) / GLA_TAU
    row = lax.broadcasted_iota(jnp.int32, (tile, tile), 0)
    col = lax.broadcasted_iota(jnp.int32, (tile, tile), 1)
    same = (row // c) == (col // c)
    fwd = d == 0
    tri = same & (jnp.where(fwd, col, row) <= jnp.where(fwd, row, col))
    tri = jnp.where(tri, 1.0, 0.0).astype(BF16)
    ones = jnp.where(same, 1.0, 0.0).astype(BF16)
    g_hi = g.astype(BF16)
    g_lo = (g - g_hi.astype(F32)).astype(BF16)
    bsum = _mm(tri, g_hi) + _mm(tri, g_lo)
    tot = _mm(ones, g_hi) + _mm(ones, g_lo)
    q = q_ref[...].astype(F32) * (GLA_HK ** -0.5)
    k = k_ref[...].astype(F32)
    qb_scr[...] = (q * jnp.exp(bsum)).astype(BF16)
    kb_scr[...] = (k * jnp.exp(-bsum)).astype(BF16)
    kd_scr[...] = (k * jnp.exp(tot - bsum)).astype(BF16)
    dec_scr[...] = jnp.exp(tot)

    def chunk_pass(forward):
        keep = same & ((col <= row) if forward else (col > row))
        order = range(n_chunks) if forward else range(n_chunks - 1, -1, -1)
        for h in range(GLA_HEADS):
            ks = slice(h * GLA_HK, (h + 1) * GLA_HK)
            vs = slice(h * GLA_HV, (h + 1) * GLA_HV)
            qb = qb_scr[:, ks]
            vh = v_ref[:, vs]
            att = jnp.where(keep, _mm_nt(qb, kb_scr[:, ks]), 0.0).astype(BF16)
            o_intra = _mm(att, vh)
            s_t = st_scr[h]
            for ci in order:
                rows = slice(ci * c, (ci + 1) * c)
                o = o_intra[rows] + _mm_nt(qb[rows], s_t.astype(BF16))
                o_ref[0, rows, vs] = o.astype(o_ref.dtype)
                s_t = s_t * dec_scr[ci * c:ci * c + 1, ks] + _mm_tn(vh[rows], kd_scr[rows, ks])
            st_scr[h] = s_t

    @pl.when(fwd)
    def _():
        chunk_pass(True)

    @pl.when(jnp.logical_not(fwd))
    def _():
        chunk_pass(False)


def _gla_scan(proj, batch, seq_len, w2p, gate_b):
    n = proj.shape[0]
    tile = _pick(seq_len, (256,))
    n_tiles = seq_len // tile

    def tok(b, d, i):
        return b * n_tiles + jnp.where(d == 0, i, n_tiles - 1 - i)

    return pl.pallas_call(
        functools.partial(_gla_kernel, tile=tile, n_tiles=n_tiles),
        grid=(batch, 2, n_tiles),
        in_specs=[pl.BlockSpec((tile, GLA_DK), lambda b, d, i: (tok(b, d, i), 0)),
                  pl.BlockSpec((tile, GLA_DK), lambda b, d, i: (tok(b, d, i), 1)),
                  pl.BlockSpec((tile, GLA_DV), lambda b, d, i: (tok(b, d, i), 1)),
                  pl.BlockSpec((tile, GLA_EXTRA), lambda b, d, i: (tok(b, d, i), (2 * GLA_DK + 2 * GLA_DV) // GLA_EXTRA)),
                  pl.BlockSpec((1, GLA_EXTRA, GLA_DK), lambda b, d, i: (d, 0, 0)),
                  pl.BlockSpec((1, 1, GLA_DK), lambda b, d, i: (d, 0, 0))],
        out_specs=pl.BlockSpec((1, tile, GLA_DV), lambda b, d, i: (d, tok(b, d, i), 0)),
        out_shape=jax.ShapeDtypeStruct((2, n, GLA_DV), BF16),
        scratch_shapes=[pltpu.VMEM((tile, GLA_DK), BF16), pltpu.VMEM((tile, GLA_DK), BF16),
                        pltpu.VMEM((tile, GLA_DK), BF16), pltpu.VMEM((tile, GLA_DK), F32),
                        pltpu.VMEM((GLA_HEADS, GLA_HV, GLA_HK), F32)],
        compiler_params=_cp("parallel", "arbitrary", "arbitrary"),
        name="gla_scan",
    )(proj, proj, proj, proj, w2p, gate_b.reshape(2, 1, GLA_DK))


def _gla_mixer(x, batch, seq_len, g, w_in, gate_w1, gate_w2, gate_b, head_norm, w_out):
    d = D_MODEL
    pad = jnp.zeros((d, GLA_EXTRA - 2 * GLA_GATE_RANK), w_in.dtype)
    w_ext = jnp.concatenate([w_in, gate_w1[0].astype(w_in.dtype), gate_w1[1].astype(w_in.dtype), pad], axis=1)
    proj = _norm_matmul(x, g, w_ext)
    w2p = jnp.zeros((2, GLA_EXTRA, GLA_DK), F32)
    w2p = w2p.at[0, :GLA_GATE_RANK].set(gate_w2[0]).at[1, GLA_GATE_RANK:2 * GLA_GATE_RANK].set(gate_w2[1])
    o = _gla_scan(proj, batch, seq_len, w2p.astype(BF16), gate_b)

    def prologue(of, ob, r, hn):
        o_sum = of.astype(F32) + ob.astype(F32)
        parts = []
        for h in range(GLA_HEADS):
            oh = o_sum[:, h * GLA_HV:(h + 1) * GLA_HV]
            parts.append(_rms(oh, hn))
        r = r.astype(F32)
        return jnp.concatenate(parts, axis=-1) * (r * _sigmoid(r))

    return _proj_res(prologue, [(o, GLA_DV, 0, 0), (o, GLA_DV, 0, 1), (proj, GLA_DV, 2, None)],
                     [head_norm.reshape(1, GLA_HV).astype(F32)], w_out, x, "gla_out")


HY_PAD = 128


def _hy_filter_kernel(bands_ref, w1t_ref, w1c_ref, w1s_ref, b1_ref, w2_ref, b2_ref, w3_ref, freq_ref, ld_ref,
                      h_ref, s_ref, *, seq_len, tile):
    i = pl.program_id(0)
    hp = lax.Precision.HIGHEST
    j = (i * tile + lax.broadcasted_iota(jnp.int32, (tile, 1), 0)).astype(F32)
    t = j / seq_len
    ang = (2.0 * math.pi / seq_len) * j * bands_ref[...]
    pre = (t * w1t_ref[...]
           + jnp.dot(jnp.cos(ang), w1c_ref[...], precision=hp, preferred_element_type=F32)
           + jnp.dot(jnp.sin(-ang), w1s_ref[...], precision=hp, preferred_element_type=F32)
           + b1_ref[...])
    h = jnp.sin(freq_ref[0:1, :] * pre)
    h = jnp.sin(freq_ref[1:2, :] * (jnp.dot(h, w2_ref[...], precision=hp, preferred_element_type=F32) + b2_ref[...]))
    h = _mm(h.astype(BF16), w3_ref[...])
    dist = jnp.abs(j - (seq_len // 2)) / (seq_len / 2)
    h = h * jnp.exp(-dist * jnp.exp(ld_ref[...]))
    h_ref[...] = h.astype(h_ref.dtype)

    @pl.when(i == 0)
    def _():
        s_ref[...] = jnp.zeros_like(s_ref)

    s_ref[...] += jnp.sum(jnp.abs(h), axis=0, keepdims=True)


def _hy_filter(seq_len, w1, b1, w2, b2, w3, freq, log_decay):
    hid = HY_FILTER_HID
    c2 = w3.shape[1]
    tile = _pick(seq_len, (512, 256))
    bands = jnp.zeros((1, HY_PAD), F32).at[0, :HY_BANDS].set(jnp.linspace(1e-4, HY_BANDS - 1, HY_BANDS, dtype=F32))
    w1 = w1.astype(F32)
    w1c = jnp.zeros((HY_PAD, hid), F32).at[:HY_BANDS].set(w1[1:1 + HY_BANDS])
    w1s = jnp.zeros((HY_PAD, hid), F32).at[:HY_BANDS].set(w1[1 + HY_BANDS:])
    full = lambda shape: pl.BlockSpec(shape, lambda i: (0, 0))
    return pl.pallas_call(
        functools.partial(_hy_filter_kernel, seq_len=seq_len, tile=tile),
        grid=(seq_len // tile,),
        in_specs=[full((1, HY_PAD)), full((1, hid)), full((HY_PAD, hid)), full((HY_PAD, hid)), full((1, hid)),
                  full((hid, hid)), full((1, hid)), full((hid, c2)), full((2, hid)), full((1, c2))],
        out_specs=[pl.BlockSpec((tile, c2), lambda i: (i, 0)), pl.BlockSpec((1, c2), lambda i: (0, 0))],
        out_shape=[jax.ShapeDtypeStruct((seq_len, c2), BF16), jax.ShapeDtypeStruct((1, c2), F32)],
        compiler_params=_cp("arbitrary"),
        name="hy_filter",
    )(bands, w1[0:1], w1c, w1s, b1.reshape(1, hid).astype(F32), w2.astype(F32), b2.reshape(1, hid).astype(F32),
      w3.astype(BF16), freq.astype(F32), log_decay.reshape(1, c2).astype(F32))


def _hy_pre_kernel(cur_ref, prev_ref, next_ref, w_ref, b_ref, o_ref, *, tile, n_tiles):
    i = pl.program_id(1)
    hl = BF16_SUBLANES
    prev = jnp.where(i == 0, 0.0, prev_ref[...].astype(F32))
    nxt = jnp.where(i == n_tiles - 1, 0.0, next_ref[...].astype(F32))
    xp = jnp.concatenate([prev, cur_ref[...].astype(F32), nxt], axis=0)
    y = b_ref[...]
    for k in range(3):
        s = hl + k - 1
        y = y + xp[s:s + tile] * w_ref[k:k + 1, :]
    o_ref[...] = y.astype(o_ref.dtype)


def _hy_pre(u, batch, seq_len, short_w, short_b):
    n, c3 = u.shape
    w = D_MODEL
    tile = _pick(seq_len, (512, 256))
    n_tiles = seq_len // tile
    hl = BF16_SUBLANES
    per_tile, per_seq = tile // hl, seq_len // hl
    return pl.pallas_call(
        functools.partial(_hy_pre_kernel, tile=tile, n_tiles=n_tiles),
        grid=(batch, n_tiles, c3 // w),
        in_specs=[pl.BlockSpec((tile, w), lambda b, i, c: (b * n_tiles + i, c)),
                  pl.BlockSpec((hl, w), lambda b, i, c: (jnp.maximum(b * per_seq + i * per_tile - 1, b * per_seq), c)),
                  pl.BlockSpec((hl, w), lambda b, i, c: (jnp.minimum(b * per_seq + (i + 1) * per_tile,
                                                                     (b + 1) * per_seq - 1), c)),
                  pl.BlockSpec((3, w), lambda b, i, c: (0, c)),
                  pl.BlockSpec((1, w), lambda b, i, c: (0, c))],
        out_specs=pl.BlockSpec((tile, w), lambda b, i, c: (b * n_tiles + i, c)),
        out_shape=jax.ShapeDtypeStruct((n, c3), BF16),
        compiler_params=_cp("parallel", "parallel", "parallel"),
        name="hy_pre",
    )(u, u, u, short_w.astype(F32), short_b.reshape(1, c3).astype(F32))


def _fft_plan(seq_len):
    n = 2 * seq_len
    n2 = 256 if n >= 32768 else 128
    n1 = n // n2
    assert n1 % 4 == 0 and n1 * n2 == n
    return n, n1, n2, n1 // 2


def _fft_tables(seq_len):
    n, n1, n2, ah = _fft_plan(seq_len)
    b = jnp.arange(n2, dtype=jnp.int32)
    ka = jnp.arange(n1, dtype=jnp.int32)
    a = jnp.arange(ah, dtype=jnp.int32)
    w = 2.0 * math.pi / n
    m = (ka[None, :, None] * (b[:, None, None] + n2 * a[None, None, :])) % n
    ang = w * m.astype(F32)
    ft = jnp.concatenate([jnp.cos(ang), -jnp.sin(ang)], axis=1).astype(BF16)
    ap = a + n1 // 4
    m = ((b[:, None, None] + n2 * ap[None, :, None]) * ka[None, None, :]) % n
    ang = w * m.astype(F32)
    fti = (jnp.concatenate([jnp.cos(ang), -jnp.sin(ang)], axis=2) / n).astype(BF16)
    ang = (2.0 * math.pi / n2) * ((b[:, None] * b[None, :]) % n2).astype(F32)
    fr, fi = jnp.cos(ang), -jnp.sin(ang)
    m2 = jnp.block([[fr, -fi], [fi, fr]]).astype(BF16)
    m2i = jnp.block([[fr, fi], [-fi, fr]]).astype(BF16)
    return ft, fti, m2, m2i


def _fft_first_kernel(z_ref, ft_ref, o_ref, *, tb):
    for i in range(tb):
        o_ref[0, i] = _mm(ft_ref[i], z_ref[0, i]).astype(o_ref.dtype)


def _fft_first(zb, ft):
    bsz, n2, ah, c = zb.shape
    rows = ft.shape[1]
    cb = 1024
    tb = _pick(n2, (16, 8))
    return pl.pallas_call(
        functools.partial(_fft_first_kernel, tb=tb),
        grid=(bsz, n2 // tb, c // cb),
        in_specs=[pl.BlockSpec((1, tb, ah, cb), lambda b, j, cc: (b, j, 0, cc)),
                  pl.BlockSpec((tb, rows, ah), lambda b, j, cc: (j, 0, 0))],
        out_specs=pl.BlockSpec((1, tb, rows, cb), lambda b, j, cc: (b, j, 0, cc)),
        out_shape=jax.ShapeDtypeStruct((bsz, n2, rows, c), BF16),
        compiler_params=_cp("parallel", "parallel", "parallel"),
        name="fft_first",
    )(zb, ft)


def _fft_spectrum_kernel(a_ref, s_ref, m_ref, o_ref, *, tk):
    for i in range(tk):
        o_ref[0, i] = (_mm(m_ref[...], a_ref[0, i]) / s_ref[...]).astype(o_ref.dtype)


def _fft_conv_kernel(a_ref, h_ref, m_ref, mi_ref, o_ref, *, tk, n2):
    for i in range(tk):
        x = _mm(m_ref[...], a_ref[0, i])
        xr, xi = x[:n2], x[n2:]
        hr, hi = h_ref[0, i, :n2].astype(F32), h_ref[0, i, n2:].astype(F32)
        p = jnp.concatenate([xr * hr - xi * hi, xr * hi + xi * hr], axis=0).astype(BF16)
        o_ref[0, i] = _mm(mi_ref[...], p).astype(o_ref.dtype)


def _fft_mid(ak, m2, m2i=None, spec=None, spec_col=0, scale=None):
    bsz, n1, r2, c = ak.shape
    cb = 1024
    tk = _pick(n1, (4, 2))
    grid = (bsz, n1 // tk, c // cb)
    a_spec = pl.BlockSpec((1, tk, r2, cb), lambda b, k, cc: (b, k, 0, cc))
    m_spec = pl.BlockSpec((r2, r2), lambda b, k, cc: (0, 0))
    if spec is None:
        return pl.pallas_call(
            functools.partial(_fft_spectrum_kernel, tk=tk),
            grid=grid,
            in_specs=[a_spec, pl.BlockSpec((1, cb), lambda b, k, cc: (0, cc)), m_spec],
            out_specs=a_spec,
            out_shape=jax.ShapeDtypeStruct(ak.shape, BF16),
            compiler_params=_cp("parallel", "parallel", "parallel"),
            name="fft_spectrum",
        )(ak, scale, m2)
    return pl.pallas_call(
        functools.partial(_fft_conv_kernel, tk=tk, n2=r2 // 2),
        grid=grid,
        in_specs=[a_spec, pl.BlockSpec((1, tk, r2, cb), lambda b, k, cc: (0, k, 0, spec_col + cc)), m_spec, m_spec],
        out_specs=a_spec,
        out_shape=jax.ShapeDtypeStruct(ak.shape, BF16),
        compiler_params=_cp("parallel", "parallel", "parallel"),
        name="fft_conv",
    )(ak, spec, m2, m2i)


def _fft_last_kernel(c_ref, fti_ref, o_ref, *, tb):
    for i in range(tb):
        o_ref[0, i] = _mm(fti_ref[i], c_ref[0, i]).astype(o_ref.dtype)


def _fft_last(cb_arr, fti):
    bsz, n2, rows, c = cb_arr.shape
    ah = fti.shape[1]
    cb = 1024
    tb = _pick(n2, (16, 8))
    return pl.pallas_call(
        functools.partial(_fft_last_kernel, tb=tb),
        grid=(bsz, n2 // tb, c // cb),
        in_specs=[pl.BlockSpec((1, tb, rows, cb), lambda b, j, cc: (b, j, 0, cc)),
                  pl.BlockSpec((tb, ah, rows), lambda b, j, cc: (j, 0, 0))],
        out_specs=pl.BlockSpec((1, tb, ah, cb), lambda b, j, cc: (b, j, 0, cc)),
        out_shape=jax.ShapeDtypeStruct((bsz, n2, ah, c), BF16),
        compiler_params=_cp("parallel", "parallel", "parallel"),
        name="fft_last",
    )(cb_arr, fti)


def _to_b_major(z, bsz, n2, ah):
    c = z.shape[-1]
    return z.reshape(bsz, ah, n2, c).transpose(0, 2, 1, 3)


def _swap_digits(arr, inner):
    bsz, p, r, c = arr.shape
    return arr.reshape(bsz, p, 2, inner, c).transpose(0, 3, 2, 1, 4).reshape(bsz, inner, 2 * p, c)


def _fft_forward(z, bsz, seq_len, ft):
    _, n1, n2, ah = _fft_plan(seq_len)
    a = _fft_first(_to_b_major(z, bsz, n2, ah), ft)
    return _swap_digits(a, n1)


def _long_conv(z, bsz, seq_len, tables, spec, order):
    ft, fti, m2, m2i = tables
    _, n1, n2, ah = _fft_plan(seq_len)
    ak = _fft_forward(z, bsz, seq_len, ft)
    ck = _fft_mid(ak, m2, m2i, spec=spec, spec_col=order * (D_MODEL // 1024))
    y = _fft_last(_swap_digits(ck, n2), fti)
    return y.transpose(0, 2, 1, 3).reshape(bsz * seq_len, z.shape[-1])


def _hy_gate_kernel(x1_ref, y_ref, v_ref, skip_ref, o_ref):
    v = v_ref[...].astype(F32)
    o_ref[...] = (x1_ref[...].astype(F32) * (y_ref[...].astype(F32) + skip_ref[...] * v)).astype(o_ref.dtype)


def _hy_gate(u3, y, skip0):
    n, w = y.shape
    tm = _pick(n, (1024, 512, 256))
    return pl.pallas_call(
        _hy_gate_kernel,
        grid=(n // tm,),
        in_specs=[pl.BlockSpec((tm, w), lambda i: (i, 1)), pl.BlockSpec((tm, w), lambda i: (i, 0)),
                  pl.BlockSpec((tm, w), lambda i: (i, 0)), pl.BlockSpec((1, w), lambda i: (0, 0))],
        out_specs=pl.BlockSpec((tm, w), lambda i: (i, 0)),
        out_shape=jax.ShapeDtypeStruct((n, w), BF16),
        compiler_params=_cp("parallel"),
        name="hy_gate",
    )(u3, y, u3, skip0.reshape(1, w).astype(F32))


def _hyena_mixer(x, batch, seq_len, g, w_in, short_w, short_b, fw1, fb1, fw2, fb2, fw3, ffreq, log_decay, skip,
                 w_out):
    d = D_MODEL
    tables = _fft_tables(seq_len)
    filt, l1 = _hy_filter(seq_len, fw1, fb1, fw2, fb2, fw3, ffreq, log_decay)
    fk = _fft_forward(filt, 1, seq_len, tables[0])
    spec = _fft_mid(fk, tables[2], scale=l1)
    u3 = _hy_pre(_norm_matmul(x, g, w_in), batch, seq_len, short_w, short_b)
    y1 = _long_conv(u3[:, :d], batch, seq_len, tables, spec, 0)
    z1 = _hy_gate(u3, y1, skip[0])
    y2 = _long_conv(z1, batch, seq_len, tables, spec, 1)

    def prologue(x2, yy, zz, sk):
        return x2.astype(F32) * (yy.astype(F32) + sk * zz.astype(F32))

    return _proj_res(prologue, [(u3, d, 2, None), (y2, d, 0, None), (z1, d, 0, None)],
                     [skip[1].reshape(1, d).astype(F32)], w_out, x, "hy_out")


def _trunk(x, mem, p):
    batch, seq_len, d = x.shape
    x = x.reshape(batch * seq_len, d)
    mem = mem.reshape(batch * MEM_TOKENS, d)
    depth = p["norm_mix"].shape[0]
    for i in range(depth):
        m, j = i % 4, i // 4
        g = p["norm_mix"][i]
        if m == 0:
            x = _na_mixer(x, batch, seq_len, g, p["na_w_qkv"][j], p["na_rpb"][j], p["na_w_out"][j])
        elif m == 1:
            x = _lru_mixer(x, batch, seq_len, g, p["lru_w_in"][j], p["lru_conv_w"][j], p["lru_conv_b"][j],
                           p["lru_gate_a_w"][j], p["lru_gate_a_b"][j], p["lru_gate_x_w"][j], p["lru_gate_x_b"][j],
                           p["lru_lambda"][j], p["lru_w_out"][j])
        elif m == 2:
            x = _gla_mixer(x, batch, seq_len, g, p["gla_w_in"][j], p["gla_gate_w1"][j], p["gla_gate_w2"][j],
                           p["gla_gate_b"][j], p["gla_head_norm"][j], p["gla_w_out"][j])
        else:
            x = _hyena_mixer(x, batch, seq_len, g, p["hy_w_in"][j], p["hy_short_w"][j], p["hy_short_b"][j],
                             p["hy_filt_w1"][j], p["hy_filt_b1"][j], p["hy_filt_w2"][j], p["hy_filt_b2"][j],
                             p["hy_filt_w3"][j], p["hy_filt_freq"][j], p["hy_log_decay"][j], p["hy_skip"][j],
                             p["hy_w_out"][j])
        kv = _norm_matmul(mem, p["norm_mem"][i], p["xa_w_kv"][i])
        x = _xattn(x, seq_len, p["norm_xattn"][i], p["xa_w_q"][i], kv, p["xa_w_out"][i])
        if i % 2 == 0:
            x = _ffn(x, p["norm_ffn"][i], p["ffn_w_gate_up"][i // 2], p["ffn_w_down"][i // 2])
        else:
            x = _moe(x, p["norm_ffn"][i], p["moe_router"][i // 2], p["moe_w_gate_up"][i // 2], p["moe_w_down"][i // 2])
    return _final_norm(x, p["norm_final"]).reshape(batch, seq_len, d)


_MATMUL_WEIGHTS = ("na_w_qkv", "na_w_out", "lru_w_in", "lru_w_out", "gla_w_in", "gla_w_out", "hy_w_in", "hy_w_out",
                   "xa_w_q", "xa_w_kv", "xa_w_out", "ffn_w_gate_up", "ffn_w_down", "moe_w_gate_up", "moe_w_down")


def kernel(x_prompt, x_sample, mem_prompt, mem_sample, norm_mix, norm_xattn, norm_mem, norm_ffn, norm_final, na_w_qkv, na_rpb, na_w_out, lru_w_in, lru_conv_w, lru_conv_b, lru_gate_a_w, lru_gate_a_b, lru_gate_x_w, lru_gate_x_b, lru_lambda, lru_w_out, gla_w_in, gla_gate_w1, gla_gate_w2, gla_gate_b, gla_head_norm, gla_w_out, hy_w_in, hy_short_w, hy_short_b, hy_filt_w1, hy_filt_b1, hy_filt_w2, hy_filt_b2, hy_filt_w3, hy_filt_freq, hy_log_decay, hy_skip, hy_w_out, xa_w_q, xa_w_kv, xa_w_out, ffn_w_gate_up, ffn_w_down, moe_router, moe_w_gate_up, moe_w_down):
    p = dict(norm_mix=norm_mix, norm_xattn=norm_xattn, norm_mem=norm_mem, norm_ffn=norm_ffn, norm_final=norm_final,
             na_w_qkv=na_w_qkv, na_rpb=na_rpb, na_w_out=na_w_out,
             lru_w_in=lru_w_in, lru_conv_w=lru_conv_w, lru_conv_b=lru_conv_b, lru_gate_a_w=lru_gate_a_w,
             lru_gate_a_b=lru_gate_a_b, lru_gate_x_w=lru_gate_x_w, lru_gate_x_b=lru_gate_x_b, lru_lambda=lru_lambda,
             lru_w_out=lru_w_out,
             gla_w_in=gla_w_in, gla_gate_w1=gla_gate_w1, gla_gate_w2=gla_gate_w2, gla_gate_b=gla_gate_b,
             gla_head_norm=gla_head_norm, gla_w_out=gla_w_out,
             hy_w_in=hy_w_in, hy_short_w=hy_short_w, hy_short_b=hy_short_b, hy_filt_w1=hy_filt_w1,
             hy_filt_b1=hy_filt_b1, hy_filt_w2=hy_filt_w2, hy_filt_b2=hy_filt_b2, hy_filt_w3=hy_filt_w3,
             hy_filt_freq=hy_filt_freq, hy_log_decay=hy_log_decay, hy_skip=hy_skip, hy_w_out=hy_w_out,
             xa_w_q=xa_w_q, xa_w_kv=xa_w_kv, xa_w_out=xa_w_out,
             ffn_w_gate_up=ffn_w_gate_up, ffn_w_down=ffn_w_down,
             moe_router=moe_router, moe_w_gate_up=moe_w_gate_up, moe_w_down=moe_w_down)
    for name in _MATMUL_WEIGHTS:
        p[name] = p[name].astype(BF16)
    return (_trunk(x_prompt, mem_prompt, p), _trunk(x_sample, mem_sample, p))
```

```python
import functools
import math

import jax
import jax.numpy as jnp
from jax import lax
from jax.experimental import pallas as pl
from jax.experimental.pallas import tpu as pltpu

F32 = jnp.float32
BF16 = jnp.bfloat16

D_MODEL = 1024
RMS_EPS = 1e-6
GRID_W = 64
NA_HEADS = 16
NA_HEAD_DIM = 64
NA_WIN_ROWS = 8
NA_WIN_COLS = 16
LRU_C = 8.0
LRU_CONV = 4
GLA_HEADS = 4
GLA_DK = 512
GLA_DV = 1024
GLA_HK = 128
GLA_HV = 256
GLA_GATE_RANK = 16
GLA_TAU = 16.0
GLA_CHUNK = 64
HY_BANDS = 16
HY_FILTER_HID = 64
MEM_TOKENS = 256
XA_HEADS = 4
XA_HEAD_DIM = 256
D_FF = 3584
N_EXPERTS = 8
NEG_BIG = -1e30

V7X_VMEM_BYTES = 64 * 1024 * 1024
VMEM_LIMIT = V7X_VMEM_BYTES - 8 * 1024 * 1024
BF16_SUBLANES = 16


def _cp(*sem):
    return pltpu.CompilerParams(dimension_semantics=sem, vmem_limit_bytes=VMEM_LIMIT)


def _pick(n, cands):
    for c in cands:
        if n % c == 0:
            return c
    return n


def _rms(x, g):
    return x * lax.rsqrt(jnp.mean(x * x, axis=-1, keepdims=True) + RMS_EPS) * g


def _sigmoid(x):
    return 1.0 / (1.0 + jnp.exp(-x))


def _softplus(x):
    return jnp.maximum(x, 0.0) + jnp.log1p(jnp.exp(-jnp.abs(x)))


def _mm(a, b):
    return jnp.dot(a, b, preferred_element_type=F32)


def _mm_nt(a, b):
    return lax.dot_general(a, b, (((1,), (1,)), ((), ())), preferred_element_type=F32)


def _mm_tn(a, b):
    return lax.dot_general(a, b, (((0,), (0,)), ((), ())), preferred_element_type=F32)


def _norm_matmul_kernel(x_ref, g_ref, w_ref, o_ref, *, tn):
    xn = _rms(x_ref[...], g_ref[...]).astype(BF16)
    for j in range(o_ref.shape[1] // tn):
        cols = slice(j * tn, (j + 1) * tn)
        o_ref[:, cols] = _mm(xn, w_ref[:, cols]).astype(o_ref.dtype)


def _norm_matmul(x, g, w, out_dtype=BF16):
    n, d = x.shape
    f = w.shape[1]
    tm = _pick(n, (1024, 512, 256))
    tn = _pick(f, (1024, 640, 512, 256, 128))
    return pl.pallas_call(
        functools.partial(_norm_matmul_kernel, tn=tn),
        grid=(n // tm,),
        in_specs=[pl.BlockSpec((tm, d), lambda i: (i, 0)),
                  pl.BlockSpec((1, d), lambda i: (0, 0)),
                  pl.BlockSpec((d, f), lambda i: (0, 0))],
        out_specs=pl.BlockSpec((tm, f), lambda i: (i, 0)),
        out_shape=jax.ShapeDtypeStruct((n, f), out_dtype),
        compiler_params=_cp("parallel"),
        name="norm_matmul",
    )(x, g.reshape(1, d), w)


def _final_norm_kernel(x_ref, g_ref, o_ref):
    o_ref[...] = _rms(x_ref[...], g_ref[...])


def _final_norm(x, g):
    n, d = x.shape
    tm = _pick(n, (1024, 512, 256))
    return pl.pallas_call(
        _final_norm_kernel,
        grid=(n // tm,),
        in_specs=[pl.BlockSpec((tm, d), lambda i: (i, 0)), pl.BlockSpec((1, d), lambda i: (0, 0))],
        out_specs=pl.BlockSpec((tm, d), lambda i: (i, 0)),
        out_shape=jax.ShapeDtypeStruct((n, d), F32),
        compiler_params=_cp("parallel"),
        name="final_norm",
    )(x, g.reshape(1, d))


def _proj_res(prologue, tok_inputs, row_inputs, w, res, name):
    n, d = res.shape
    k = w.shape[0]
    tm = _pick(n, (1024, 512, 256))
    n_tok, n_row = len(tok_inputs), len(row_inputs)

    def kern(*refs):
        toks = []
        for r, (_, _, _, lead) in zip(refs[:n_tok], tok_inputs):
            toks.append(r[0] if lead is not None else r[...])
        rows = [r[...] for r in refs[n_tok:n_tok + n_row]]
        w_ref, res_ref, o_ref = refs[n_tok + n_row:]
        a = prologue(*toks, *rows)
        o_ref[...] = res_ref[...] + _mm(a.astype(BF16), w_ref[...])

    in_specs, args = [], []
    for arr, width, col, lead in tok_inputs:
        if lead is None:
            in_specs.append(pl.BlockSpec((tm, width), lambda i, c=col: (i, c)))
        else:
            in_specs.append(pl.BlockSpec((1, tm, width), lambda i, c=col, l=lead: (l, i, c)))
        args.append(arr)
    for arr in row_inputs:
        in_specs.append(pl.BlockSpec(arr.shape, lambda i: (0, 0)))
        args.append(arr)
    in_specs += [pl.BlockSpec((k, d), lambda i: (0, 0)), pl.BlockSpec((tm, d), lambda i: (i, 0))]
    args += [w, res]
    return pl.pallas_call(
        kern,
        grid=(n // tm,),
        in_specs=in_specs,
        out_specs=pl.BlockSpec((tm, d), lambda i: (i, 0)),
        out_shape=jax.ShapeDtypeStruct((n, d), F32),
        compiler_params=_cp("parallel"),
        name=name,
    )(*args)


def _xattn_kernel(x_ref, g_ref, wq_ref, k_ref, v_ref, wo_ref, o_ref):
    x = x_ref[...]
    xn = _rms(x, g_ref[...]).astype(BF16)
    q = (_mm(xn, wq_ref[...]) * (XA_HEAD_DIM ** -0.5)).astype(BF16)
    outs = []
    for h in range(XA_HEADS):
        sl = slice(h * XA_HEAD_DIM, (h + 1) * XA_HEAD_DIM)
        s = _mm_nt(q[:, sl], k_ref[:, sl])
        p = jnp.exp(s - jnp.max(s, axis=-1, keepdims=True))
        p = p / jnp.sum(p, axis=-1, keepdims=True)
        outs.append(_mm(p.astype(BF16), v_ref[:, sl]))
    o = jnp.concatenate(outs, axis=-1).astype(BF16)
    o_ref[...] = x + _mm(o, wo_ref[...])


def _xattn(x, seq_len, g, wq, kv, wo):
    n, d = x.shape
    tm = _pick(seq_len, (512, 256))
    per_b = seq_len // tm
    m = MEM_TOKENS
    return pl.pallas_call(
        _xattn_kernel,
        grid=(n // tm,),
        in_specs=[pl.BlockSpec((tm, d), lambda i: (i, 0)),
                  pl.BlockSpec((1, d), lambda i: (0, 0)),
                  pl.BlockSpec((d, d), lambda i: (0, 0)),
                  pl.BlockSpec((m, d), lambda i: (i // per_b, 0)),
                  pl.BlockSpec((m, d), lambda i: (i // per_b, 1)),
                  pl.BlockSpec((d, d), lambda i: (0, 0))],
        out_specs=pl.BlockSpec((tm, d), lambda i: (i, 0)),
        out_shape=jax.ShapeDtypeStruct((n, d), F32),
        compiler_params=_cp("parallel"),
        name="xattn",
    )(x, g.reshape(1, d), wq, kv, kv, wo)


def _ffn_kernel(x_ref, g_ref, wg_ref, wu_ref, wd_ref, o_ref, xn_ref, acc_ref):
    f = pl.program_id(1)

    @pl.when(f == 0)
    def _():
        xn_ref[...] = _rms(x_ref[...], g_ref[...]).astype(BF16)
        acc_ref[...] = jnp.zeros_like(acc_ref)

    xn = xn_ref[...]
    hg = _mm(xn, wg_ref[...])
    hu = _mm(xn, wu_ref[...])
    a = (hg * _sigmoid(hg) * hu).astype(BF16)
    acc_ref[...] += _mm(a, wd_ref[...])

    @pl.when(f == pl.num_programs(1) - 1)
    def _():
        o_ref[...] = x_ref[...] + acc_ref[...]


def _ffn(x, g, w_gu, w_d):
    n, d = x.shape
    ff = w_d.shape[0]
    tm = _pick(n, (1024, 512, 256))
    tf = _pick(ff, (512, 256, 128))
    nf = ff // tf
    return pl.pallas_call(
        _ffn_kernel,
        grid=(n // tm, nf),
        in_specs=[pl.BlockSpec((tm, d), lambda i, f: (i, 0)),
                  pl.BlockSpec((1, d), lambda i, f: (0, 0)),
                  pl.BlockSpec((d, tf), lambda i, f: (0, f)),
                  pl.BlockSpec((d, tf), lambda i, f: (0, f + nf)),
                  pl.BlockSpec((tf, d), lambda i, f: (f, 0))],
        out_specs=pl.BlockSpec((tm, d), lambda i, f: (i, 0)),
        out_shape=jax.ShapeDtypeStruct((n, d), F32),
        scratch_shapes=[pltpu.VMEM((tm, d), BF16), pltpu.VMEM((tm, d), F32)],
        compiler_params=_cp("parallel", "arbitrary"),
        name="ffn",
    )(x, g.reshape(1, d), w_gu, w_gu, w_d)


def _router_kernel(x_ref, g_ref, wh_ref, wl_ref, idx_ref, wts_ref):
    xn = _rms(x_ref[...], g_ref[...])
    xh = xn.astype(BF16)
    xl = (xn - xh.astype(F32)).astype(BF16)
    logits = _mm(xh, wh_ref[...]) + _mm(xh, wl_ref[...]) + _mm(xl, wh_ref[...])
    lane = lax.broadcasted_iota(jnp.int32, logits.shape, 1)
    logits = jnp.where(lane < N_EXPERTS, logits, NEG_BIG)
    m1 = jnp.max(logits, axis=-1, keepdims=True)
    i1 = jnp.min(jnp.where(logits == m1, lane, 128), axis=-1, keepdims=True)
    rest = jnp.where(lane == i1, NEG_BIG, logits)
    m2 = jnp.max(rest, axis=-1, keepdims=True)
    i2 = jnp.min(jnp.where(rest == m2, lane, 128), axis=-1, keepdims=True)
    e2 = jnp.exp(m2 - m1)
    den = 1.0 + e2
    idx_ref[...] = jnp.where(lane == 0, i1, jnp.where(lane == 1, i2, 0))
    wts_ref[...] = jnp.where(lane == 0, 1.0 / den, jnp.where(lane == 1, e2 / den, 0.0))


def _router(x, g, w_router):
    n, d = x.shape
    tm = _pick(n, (1024, 512, 256))
    wp = jnp.zeros((d, 128), F32).at[:, :N_EXPERTS].set(w_router)
    wh = wp.astype(BF16)
    wl = (wp - wh.astype(F32)).astype(BF16)
    return pl.pallas_call(
        _router_kernel,
        grid=(n // tm,),
        in_specs=[pl.BlockSpec((tm, d), lambda i: (i, 0)),
                  pl.BlockSpec((1, d), lambda i: (0, 0)),
                  pl.BlockSpec((d, 128), lambda i: (0, 0)),
                  pl.BlockSpec((d, 128), lambda i: (0, 0))],
        out_specs=[pl.BlockSpec((tm, 128), lambda i: (i, 0)), pl.BlockSpec((tm, 128), lambda i: (i, 0))],
        out_shape=[jax.ShapeDtypeStruct((n, 128), jnp.int32), jax.ShapeDtypeStruct((n, 128), F32)],
        compiler_params=_cp("parallel"),
        name="router",
    )(x, g.reshape(1, d), wh, wl)


F32_SUBLANES = 8


def _row_copy(src_hbm, row, buf, sem, slot, grp, sub):
    return pltpu.make_async_copy(src_hbm.at[pl.ds(row, 1), :], buf.at[slot, grp, pl.ds(sub, 1), :], sem.at[slot])


def _row_gather_groups(src_hbm, idx_ref, buf, sem, slot, first_grp, n_groups):
    for g in range(n_groups):
        for sub in range(F32_SUBLANES):
            row = idx_ref[(first_grp + g) * F32_SUBLANES + sub]
            _row_copy(src_hbm, row, buf, sem, slot, first_grp + g, sub).start()


def _row_gather_start(src_hbm, idx_ref, buf, sem, slot, n_rows):
    def body(g, carry):
        _row_gather_groups(src_hbm, idx_ref, buf, sem, slot, g, 1)
        return carry

    lax.fori_loop(0, n_rows // F32_SUBLANES, body, 0, unroll=2)


def _row_gather_wait(buf, sem, slot):
    pltpu.make_async_copy(buf.at[slot], buf.at[slot], sem.at[slot]).wait()


def _moe_expert_kernel(te_ref, na_ref, src_cur_ref, src_nxt_ref, x_hbm, g_ref, wg_ref, wu_ref, wd_ref, o_ref,
                       buf, xg_ref, acc_ref, sem, *, tm, nf):
    i = pl.program_id(0)
    f = pl.program_id(1)
    n_active = na_ref[0]
    active = i < n_active
    slot = i % 2
    d = xg_ref.shape[1]
    n_groups = tm // F32_SUBLANES
    groups_main = -(-n_groups // nf)
    groups_last = n_groups - groups_main * (nf - 1)
    assert nf >= 3 and groups_last > 0

    @pl.when((f == 0) & (i == 0))
    def _():
        _row_gather_start(x_hbm, src_cur_ref, buf, sem, 0, tm)

    @pl.when((f == 0) & (i <= n_active))
    def _():
        _row_gather_wait(buf, sem, slot)

    @pl.when((f == 0) & active)
    def _():
        xg_ref[...] = _rms(buf[slot].reshape(tm, d), g_ref[...]).astype(BF16)

    def step(n_issue, sink):
        _row_gather_groups(x_hbm, src_nxt_ref, buf, sem, 1 - slot, f * groups_main, n_issue)
        xn = xg_ref[...]
        hg = _mm(xn, wg_ref[0])
        hu = _mm(xn, wu_ref[0])
        a = (hg * _sigmoid(hg) * hu).astype(BF16)
        sink(_mm(a, wd_ref[0]))

    def set_acc(y):
        acc_ref[...] = y

    def add_acc(y):
        acc_ref[...] += y

    def emit(y):
        o_ref[...] = acc_ref[...] + y

    pl.when(active & (f == 0))(lambda: step(groups_main, set_acc))
    pl.when(active & (f > 0) & (f < nf - 1))(lambda: step(groups_main, add_acc))
    pl.when(active & (f == nf - 1))(lambda: step(groups_last, emit))

    @pl.when((f == nf - 1) & jnp.logical_not(active))
    def _():
        o_ref[...] = jnp.zeros_like(o_ref)


def _moe_combine_kernel(pos_cur_ref, pos_nxt_ref, x_ref, w_ref, y_hbm, o_ref, buf, sem, *, tc):
    i = pl.program_id(0)
    slot = i % 2

    @pl.when(i == 0)
    def _():
        _row_gather_start(y_hbm, pos_cur_ref, buf, sem, 0, 2 * tc)

    _row_gather_wait(buf, sem, slot)

    @pl.when(i + 1 < pl.num_programs(0))
    def _():
        _row_gather_start(y_hbm, pos_nxt_ref, buf, sem, 1 - slot, 2 * tc)

    w = w_ref[...]
    d = x_ref.shape[1]
    half = tc // F32_SUBLANES
    y0 = buf[slot, 0:half].reshape(tc, d)
    y1 = buf[slot, half:2 * half].reshape(tc, d)
    o_ref[...] = x_ref[...] + w[:, 0:1] * y0 + w[:, 1:2] * y1


def _invert_rows_kernel(pos_ref, src_ref, *, blk, clear_steps):
    s = pl.program_id(0)

    @pl.when(s < clear_steps)
    def _():
        def clear(r, carry):
            src_ref[s * blk + r] = 0
            return carry

        lax.fori_loop(0, blk, clear, 0, unroll=8)

    @pl.when(s >= clear_steps)
    def _():
        base = (s - clear_steps) * blk

        def place(a, carry):
            src_ref[pos_ref[a]] = (base + a) // 2
            return carry

        lax.fori_loop(0, blk, place, 0, unroll=8)


def _invert_rows(pos, n_rows):
    n_assign = pos.shape[0]
    blk = min(8192, n_assign)
    assert n_rows % blk == 0 and n_assign % blk == 0
    clear_steps = n_rows // blk
    return pl.pallas_call(
        functools.partial(_invert_rows_kernel, blk=blk, clear_steps=clear_steps),
        grid=(clear_steps + n_assign // blk,),
        in_specs=[pl.BlockSpec((blk,), lambda s: (jnp.maximum(s - clear_steps, 0),), memory_space=pltpu.SMEM)],
        out_specs=pl.BlockSpec((n_rows,), lambda s: (0,), memory_space=pltpu.SMEM),
        out_shape=jax.ShapeDtypeStruct((n_rows,), jnp.int32),
        compiler_params=_cp("arbitrary"),
        name="moe_invert_rows",
    )(pos)


MOE_ROW_TILE = 1024
MOE_COMBINE_TILE = 512


def _moe_plan(e_idx, tm, tc):
    n = e_idx.shape[0]
    n_assign = 2 * n
    max_tiles = n_assign // tm + N_EXPERTS
    e_flat = e_idx.reshape(n_assign)
    onehot = (e_flat[:, None] == jnp.arange(N_EXPERTS, dtype=jnp.int32)[None, :]).astype(jnp.int32)
    csum = jnp.cumsum(onehot, axis=0)
    rank = jnp.sum(onehot * csum, axis=1) - 1
    counts = csum[-1]
    padded = ((counts + tm - 1) // tm) * tm
    gend = jnp.cumsum(padded)
    pos = (gend - padded)[e_flat] + rank
    n_active = (gend[-1] // tm).astype(jnp.int32)
    tile_start = jnp.arange(max_tiles, dtype=jnp.int32) * tm
    tile_expert = jnp.sum((tile_start[:, None] >= gend[None, :]).astype(jnp.int32), axis=1)
    last_expert = jnp.sum((tile_start[jnp.maximum(n_active - 1, 0)] >= gend).astype(jnp.int32))
    tile_expert = jnp.where(tile_start < gend[-1], tile_expert, last_expert).astype(jnp.int32)
    src = _invert_rows(pos.astype(jnp.int32), max_tiles * tm)
    pos_tiles = pos.reshape(n // tc, tc, 2).transpose(0, 2, 1).reshape(n_assign)
    return src, tile_expert, n_active.reshape(1), pos_tiles.astype(jnp.int32), max_tiles


def _moe(x, g, w_router, w_gu, w_d):
    n, d = x.shape
    ff = w_d.shape[1]
    tm = min(MOE_ROW_TILE, n)
    tc = min(MOE_COMBINE_TILE, n // 2)
    tf = _pick(ff, (512, 256, 128))
    nf = ff // tf
    assert (2 * n) % tm == 0 and tm % F32_SUBLANES == 0
    idx, wts = _router(x, g, w_router)
    src, tile_expert, n_active, pos_tiles, max_tiles = _moe_plan(idx[:, :2], tm, tc)

    def wmap(col_off):
        def f(i, f_, te, na):
            return (te[i], 0, jnp.where(i < na[0], f_, nf - 1) + col_off)
        return f

    smem = functools.partial(pl.BlockSpec, memory_space=pltpu.SMEM)
    y = pl.pallas_call(
        functools.partial(_moe_expert_kernel, tm=tm, nf=nf),
        grid_spec=pltpu.PrefetchScalarGridSpec(
            num_scalar_prefetch=2,
            grid=(max_tiles, nf),
            in_specs=[smem((tm,), lambda i, f_, te, na: (i,)),
                      smem((tm,), lambda i, f_, te, na: (jnp.minimum(i + 1, max_tiles - 1),)),
                      pl.BlockSpec(memory_space=pl.ANY),
                      pl.BlockSpec((1, d), lambda i, f_, te, na: (0, 0)),
                      pl.BlockSpec((1, d, tf), wmap(0)),
                      pl.BlockSpec((1, d, tf), wmap(nf)),
                      pl.BlockSpec((1, tf, d), lambda i, f_, te, na: (te[i], jnp.where(i < na[0], f_, nf - 1), 0))],
            out_specs=pl.BlockSpec((tm, d), lambda i, f_, te, na: (i, 0)),
            scratch_shapes=[pltpu.VMEM((2, tm // F32_SUBLANES, F32_SUBLANES, d), F32), pltpu.VMEM((tm, d), BF16),
                            pltpu.VMEM((tm, d), F32), pltpu.SemaphoreType.DMA((2,))]),
        out_shape=jax.ShapeDtypeStruct((max_tiles * tm, d), F32),
        compiler_params=_cp("arbitrary", "arbitrary"),
        name="moe_experts",
    )(tile_expert, n_active, src, src, x, g.reshape(1, d), w_gu, w_gu, w_d)

    n_steps = n // tc
    return pl.pallas_call(
        functools.partial(_moe_combine_kernel, tc=tc),
        grid=(n_steps,),
        in_specs=[smem((2 * tc,), lambda i: (i,)),
                  smem((2 * tc,), lambda i: (jnp.minimum(i + 1, n_steps - 1),)),
                  pl.BlockSpec((tc, d), lambda i: (i, 0)),
                  pl.BlockSpec((tc, 128), lambda i: (i, 0)),
                  pl.BlockSpec(memory_space=pl.ANY)],
        out_specs=pl.BlockSpec((tc, d), lambda i: (i, 0)),
        out_shape=jax.ShapeDtypeStruct((n, d), F32),
        scratch_shapes=[pltpu.VMEM((2, 2 * tc // F32_SUBLANES, F32_SUBLANES, d), F32), pltpu.SemaphoreType.DMA((2,))],
        compiler_params=_cp("arbitrary"),
        name="moe_combine",
    )(pos_tiles, pos_tiles, x, wts, y)


NA_GROUP = 4
NA_KEYS = NA_WIN_ROWS * GRID_W
NA_BLK = 8 * GRID_W


def _na_bias_table(rpb):
    c = jnp.arange(GRID_W)
    cs = jnp.clip(c - NA_WIN_COLS // 2, 0, GRID_W - NA_WIN_COLS)
    cp = jnp.arange(GRID_W)
    valid = (cp[None, :] >= cs[:, None]) & (cp[None, :] < cs[:, None] + NA_WIN_COLS)
    colrel = jnp.clip(cp[None, :] - c[:, None] + NA_WIN_COLS - 1, 0, 2 * NA_WIN_COLS - 2)
    rr = jnp.arange(NA_WIN_ROWS)[:, None] + jnp.arange(NA_WIN_ROWS)[None, :]
    tbl = rpb.astype(F32)[:, rr][:, :, :, colrel]
    tbl = jnp.where(valid[None, None, None], tbl, NEG_BIG)
    tbl = tbl.reshape(NA_HEADS // NA_GROUP, NA_GROUP, NA_WIN_ROWS, NA_WIN_ROWS, GRID_W, GRID_W)
    tbl = tbl.transpose(2, 0, 3, 5, 1, 4)
    return tbl.reshape(NA_WIN_ROWS, NA_HEADS // NA_GROUP, NA_KEYS, NA_GROUP * GRID_W)


def _na_kernel(q_ref, kp_ref, kc_ref, kn_ref, vp_ref, vc_ref, vn_ref, bias_ref, o_ref, kst, vst, *, rows):
    r = pl.program_id(0) % rows
    i = r // 8

    @pl.when(r % 8 == 0)
    def _():
        kst[0:NA_BLK] = kp_ref[...]
        kst[NA_BLK:2 * NA_BLK] = kc_ref[...]
        kst[2 * NA_BLK:3 * NA_BLK] = kn_ref[...]
        vst[0:NA_BLK] = vp_ref[...]
        vst[NA_BLK:2 * NA_BLK] = vc_ref[...]
        vst[2 * NA_BLK:3 * NA_BLK] = vn_ref[...]

    rs = jnp.clip(r - NA_WIN_ROWS // 2, 0, rows - NA_WIN_ROWS)
    off = pl.multiple_of((rs - 8 * (i - 1)) * GRID_W, GRID_W)
    gw = NA_GROUP * NA_HEAD_DIM
    lane_head = lax.broadcasted_iota(jnp.int32, (GRID_W, gw), 1) // NA_HEAD_DIM
    for grp in range(NA_HEADS // NA_GROUP):
        sl = slice(grp * gw, (grp + 1) * gw)
        qg = q_ref[:, sl] * (NA_HEAD_DIM ** -0.5)
        bq = jnp.concatenate([jnp.where(lane_head == h, qg, jnp.zeros_like(qg)) for h in range(NA_GROUP)], axis=0)
        kw = kst[pl.ds(off, NA_KEYS), sl]
        st = _mm_nt(kw, bq) + bias_ref[0, grp]
        p = jnp.exp(st - jnp.max(st, axis=0, keepdims=True))
        p = (p / jnp.sum(p, axis=0, keepdims=True)).astype(BF16)
        vw = vst[pl.ds(off, NA_KEYS), sl]
        res = _mm_tn(p, vw)
        o = jnp.zeros((GRID_W, gw), F32)
        for h in range(NA_GROUP):
            o = o + jnp.where(lane_head == h, res[h * GRID_W:(h + 1) * GRID_W], 0.0)
        o_ref[:, sl] = o.astype(o_ref.dtype)


def _na_attention(qkv, batch, seq_len, bias_tbl):
    n = qkv.shape[0]
    d = D_MODEL
    rows = seq_len // GRID_W
    assert rows % 8 == 0 and rows >= 16
    nb = rows // 8

    def kmap(delta, col):
        def f(g):
            b = g // rows
            i = (g % rows) // 8
            return (b * nb + jnp.clip(i + delta, 0, nb - 1), col)
        return f

    def bias_map(g):
        r = g % rows
        return (jnp.clip(r - NA_WIN_ROWS // 2, 0, rows - NA_WIN_ROWS) - r + NA_WIN_ROWS - 1, 0, 0, 0)

    kv_specs = [pl.BlockSpec((NA_BLK, d), kmap(dl, col)) for col in (1, 2) for dl in (-1, 0, 1)]
    return pl.pallas_call(
        functools.partial(_na_kernel, rows=rows),
        grid=(batch * rows,),
        in_specs=[pl.BlockSpec((GRID_W, d), lambda g: (g, 0))] + kv_specs
                 + [pl.BlockSpec((1,) + bias_tbl.shape[1:], bias_map)],
        out_specs=pl.BlockSpec((GRID_W, d), lambda g: (g, 0)),
        out_shape=jax.ShapeDtypeStruct((n, d), BF16),
        scratch_shapes=[pltpu.VMEM((3 * NA_BLK, d), BF16), pltpu.VMEM((3 * NA_BLK, d), BF16)],
        compiler_params=_cp("arbitrary"),
        name="na_attention",
    )(qkv, qkv, qkv, qkv, qkv, qkv, qkv, bias_tbl)


def _na_mixer(x, batch, seq_len, g, w_qkv, rpb, w_out):
    qkv = _norm_matmul(x, g, w_qkv)
    o = _na_attention(qkv, batch, seq_len, _na_bias_table(rpb))
    return _proj_res(lambda a: a, [(o, D_MODEL, 0, None)], [], w_out, x, "na_out")


LRU_LANES = 128


def _halo_maps(seq_len, tile, halo, col, tile_of):
    per_tile = tile // halo
    per_seq = seq_len // halo

    def prev(*g):
        return (jnp.maximum(g[0] * per_seq + tile_of(*g) * per_tile - 1, g[0] * per_seq), col)

    def nxt(*g):
        return (jnp.minimum(g[0] * per_seq + (tile_of(*g) + 1) * per_tile, (g[0] + 1) * per_seq - 1), col)

    return prev, nxt


def _lru_kernel(cur_ref, prev_ref, next_ref, cw_ref, cb_ref, wg_ref, ba_ref, bx_ref, lam_ref, o_ref,
                a_scr, b_scr, h_scr, carry, *, tile, n_tiles):
    d = pl.program_id(1)
    i = pl.program_id(2)
    ti = jnp.where(d == 0, i, n_tiles - 1 - i)
    hl = BF16_SUBLANES
    prev = jnp.where(ti == 0, 0.0, prev_ref[...].astype(F32))
    nxt = jnp.where(ti == n_tiles - 1, 0.0, next_ref[...].astype(F32))
    xp = jnp.concatenate([prev, cur_ref[...].astype(F32), nxt], axis=0)
    left = LRU_CONV // 2
    xc = cb_ref[...]
    for k in range(LRU_CONV):
        s = hl + k - left
        xc = xc + xp[s:s + tile] * cw_ref[k:k + 1, :]
    xcb = xc.astype(BF16)
    ga, gx = [], []
    for grp in range(D_MODEL // LRU_LANES):
        gg = _mm(xcb[:, grp * LRU_LANES:(grp + 1) * LRU_LANES], wg_ref[0, grp])
        ga.append(gg[:, :LRU_LANES])
        gx.append(gg[:, LRU_LANES:])
    ga = jnp.concatenate(ga, axis=-1) + ba_ref[0]
    gx = jnp.concatenate(gx, axis=-1) + bx_ref[0]
    log_a = -LRU_C * _sigmoid(ga) * _softplus(-lam_ref[0])
    a = jnp.exp(log_a)
    a_scr[...] = a
    b_scr[...] = jnp.sqrt(1.0 - a * a) * (_sigmoid(gx) * xc)

    @pl.when(i == 0)
    def _():
        carry[...] = jnp.zeros_like(carry)

    def body(t, h):
        tt = jnp.where(d == 0, t, tile - 1 - t)
        h = a_scr[pl.ds(tt, 1), :] * h + b_scr[pl.ds(tt, 1), :]
        h_scr[pl.ds(tt, 1), :] = h
        return h

    carry[...] = lax.fori_loop(0, tile, body, carry[...], unroll=8)
    o_ref[0] = h_scr[...].astype(o_ref.dtype)


def _lru_gate_weights(ga_w, gx_w):
    def bd(w):
        w = w.reshape(2, D_MODEL // LRU_LANES, 2, 64, 64)
        z = jnp.zeros_like(w[:, :, 0])
        top = jnp.concatenate([w[:, :, 0], z], axis=-1)
        bot = jnp.concatenate([z, w[:, :, 1]], axis=-1)
        return jnp.concatenate([top, bot], axis=-2)
    return jnp.concatenate([bd(ga_w), bd(gx_w)], axis=-1).astype(BF16)


def _lru_scan(gb, batch, seq_len, conv_w, conv_b, wg, ba, bx, lam):
    n = gb.shape[0]
    w = D_MODEL
    tile = _pick(seq_len, (512, 256))
    n_tiles = seq_len // tile
    hl = BF16_SUBLANES

    def tile_of(b, d, i):
        return jnp.where(d == 0, i, n_tiles - 1 - i)

    prev_map, next_map = _halo_maps(seq_len, tile, hl, 1, tile_of)
    row = lambda b, d, i: (d, 0, 0)
    return pl.pallas_call(
        functools.partial(_lru_kernel, tile=tile, n_tiles=n_tiles),
        grid=(batch, 2, n_tiles),
        in_specs=[pl.BlockSpec((tile, w), lambda b, d, i: (b * n_tiles + tile_of(b, d, i), 1)),
                  pl.BlockSpec((hl, w), prev_map),
                  pl.BlockSpec((hl, w), next_map),
                  pl.BlockSpec((LRU_CONV, w), lambda b, d, i: (0, 0)),
                  pl.BlockSpec((1, w), lambda b, d, i: (0, 0)),
                  pl.BlockSpec((1,) + wg.shape[1:], lambda b, d, i: (d, 0, 0, 0)),
                  pl.BlockSpec((1, 1, w), row), pl.BlockSpec((1, 1, w), row), pl.BlockSpec((1, 1, w), row)],
        out_specs=pl.BlockSpec((1, tile, w), lambda b, d, i: (d, b * n_tiles + tile_of(b, d, i), 0)),
        out_shape=jax.ShapeDtypeStruct((2, n, w), BF16),
        scratch_shapes=[pltpu.VMEM((tile, w), F32), pltpu.VMEM((tile, w), F32), pltpu.VMEM((tile, w), F32),
                        pltpu.VMEM((1, w), F32)],
        compiler_params=_cp("parallel", "arbitrary", "arbitrary"),
        name="lru_scan",
    )(gb, gb, gb, conv_w, conv_b.reshape(1, w), wg, ba.reshape(2, 1, w), bx.reshape(2, 1, w), lam.reshape(2, 1, w))


def _gelu_tanh(x):
    return 0.5 * x * (1.0 + jnp.tanh(math.sqrt(2.0 / math.pi) * (x + 0.044715 * (x * x * x))))


def _lru_mixer(x, batch, seq_len, g, w_in, conv_w, conv_b, ga_w, ga_b, gx_w, gx_b, lam, w_out):
    gb = _norm_matmul(x, g, w_in)
    h = _lru_scan(gb, batch, seq_len, conv_w, conv_b, _lru_gate_weights(ga_w, gx_w), ga_b, gx_b, lam)

    def prologue(gate, hf, hb):
        return _gelu_tanh(gate.astype(F32)) * (hf.astype(F32) + hb.astype(F32))

    return _proj_res(prologue, [(gb, D_MODEL, 0, None), (h, D_MODEL, 0, 0), (h, D_MODEL, 0, 1)], [], w_out, x,
                     "lru_out")


GLA_EXTRA = 128


def _gla_kernel(q_ref, k_ref, v_ref, lr_ref, w2_ref, gb_ref, o_ref, qb_scr, kb_scr, kd_scr, dec_scr, st_scr,
                *, tile, n_tiles):
    d = pl.program_id(1)
    i = pl.program_id(2)
    c = GLA_CHUNK
    n_chunks = tile // c

    @pl.when(i == 0)
    def _():
        st_scr[...] = jnp.zeros_like(st_scr)

    z = _mm(lr_ref[...], w2_ref[0]) + gb_ref[0]
    g = -_softplus(-z) / GLA_TAU
    row = lax.broadcasted_iota(jnp.int32, (tile, tile), 0)
    col = lax.broadcasted_iota(jnp.int32, (tile, tile), 1)
    same = (row // c) == (col // c)
    fwd = d == 0
    tri = same & (jnp.where(fwd, col, row) <= jnp.where(fwd, row, col))
    tri = jnp.where(tri, 1.0, 0.0).astype(BF16)
    ones = jnp.where(same, 1.0, 0.0).astype(BF16)
    g_hi = g.astype(BF16)
    g_lo = (g - g_hi.astype(F32)).astype(BF16)
    bsum = _mm(tri, g_hi) + _mm(tri, g_lo)
    tot = _mm(ones, g_hi) + _mm(ones, g_lo)
    q = q_ref[...].astype(F32) * (GLA_HK ** -0.5)
    k = k_ref[...].astype(F32)
    qb_scr[...] = (q * jnp.exp(bsum)).astype(BF16)
    kb_scr[...] = (k * jnp.exp(-bsum)).astype(BF16)
    kd_scr[...] = (k * jnp.exp(tot - bsum)).astype(BF16)
    dec_scr[...] = jnp.exp(tot)

    def chunk_pass(forward):
        keep = same & ((col <= row) if forward else (col > row))
        order = range(n_chunks) if forward else range(n_chunks - 1, -1, -1)
        for h in range(GLA_HEADS):
            ks = slice(h * GLA_HK, (h + 1) * GLA_HK)
            vs = slice(h * GLA_HV, (h + 1) * GLA_HV)
            qb = qb_scr[:, ks]
            vh = v_ref[:, vs]
            att = jnp.where(keep, _mm_nt(qb, kb_scr[:, ks]), 0.0).astype(BF16)
            o_intra = _mm(att, vh)
            s_t = st_scr[h]
            for ci in order:
                rows = slice(ci * c, (ci + 1) * c)
                o = o_intra[rows] + _mm_nt(qb[rows], s_t.astype(BF16))
                o_ref[0, rows, vs] = o.astype(o_ref.dtype)
                s_t = s_t * dec_scr[ci * c:ci * c + 1, ks] + _mm_tn(vh[rows], kd_scr[rows, ks])
            st_scr[h] = s_t

    @pl.when(fwd)
    def _():
        chunk_pass(True)

    @pl.when(jnp.logical_not(fwd))
    def _():
        chunk_pass(False)


def _gla_scan(proj, batch, seq_len, w2p, gate_b):
    n = proj.shape[0]
    tile = _pick(seq_len, (256,))
    n_tiles = seq_len // tile

    def tok(b, d, i):
        return b * n_tiles + jnp.where(d == 0, i, n_tiles - 1 - i)

    return pl.pallas_call(
        functools.partial(_gla_kernel, tile=tile, n_tiles=n_tiles),
        grid=(batch, 2, n_tiles),
        in_specs=[pl.BlockSpec((tile, GLA_DK), lambda b, d, i: (tok(b, d, i), 0)),
                  pl.BlockSpec((tile, GLA_DK), lambda b, d, i: (tok(b, d, i), 1)),
                  pl.BlockSpec((tile, GLA_DV), lambda b, d, i: (tok(b, d, i), 1)),
                  pl.BlockSpec((tile, GLA_EXTRA), lambda b, d, i: (tok(b, d, i), (2 * GLA_DK + 2 * GLA_DV) // GLA_EXTRA)),
                  pl.BlockSpec((1, GLA_EXTRA, GLA_DK), lambda b, d, i: (d, 0, 0)),
                  pl.BlockSpec((1, 1, GLA_DK), lambda b, d, i: (d, 0, 0))],
        out_specs=pl.BlockSpec((1, tile, GLA_DV), lambda b, d, i: (d, tok(b, d, i), 0)),
        out_shape=jax.ShapeDtypeStruct((2, n, GLA_DV), BF16),
        scratch_shapes=[pltpu.VMEM((tile, GLA_DK), BF16), pltpu.VMEM((tile, GLA_DK), BF16),
                        pltpu.VMEM((tile, GLA_DK), BF16), pltpu.VMEM((tile, GLA_DK), F32),
                        pltpu.VMEM((GLA_HEADS, GLA_HV, GLA_HK), F32)],
        compiler_params=_cp("parallel", "arbitrary", "arbitrary"),
        name="gla_scan",
    )(proj, proj, proj, proj, w2p, gate_b.reshape(2, 1, GLA_DK))


def _gla_mixer(x, batch, seq_len, g, w_in, gate_w1, gate_w2, gate_b, head_norm, w_out):
    d = D_MODEL
    pad = jnp.zeros((d, GLA_EXTRA - 2 * GLA_GATE_RANK), w_in.dtype)
    w_ext = jnp.concatenate([w_in, gate_w1[0].astype(w_in.dtype), gate_w1[1].astype(w_in.dtype), pad], axis=1)
    proj = _norm_matmul(x, g, w_ext)
    w2p = jnp.zeros((2, GLA_EXTRA, GLA_DK), F32)
    w2p = w2p.at[0, :GLA_GATE_RANK].set(gate_w2[0]).at[1, GLA_GATE_RANK:2 * GLA_GATE_RANK].set(gate_w2[1])
    o = _gla_scan(proj, batch, seq_len, w2p.astype(BF16), gate_b)

    def prologue(of, ob, r, hn):
        o_sum = of.astype(F32) + ob.astype(F32)
        parts = []
        for h in range(GLA_HEADS):
            oh = o_sum[:, h * GLA_HV:(h + 1) * GLA_HV]
            parts.append(_rms(oh, hn))
        r = r.astype(F32)
        return jnp.concatenate(parts, axis=-1) * (r * _sigmoid(r))

    return _proj_res(prologue, [(o, GLA_DV, 0, 0), (o, GLA_DV, 0, 1), (proj, GLA_DV, 2, None)],
                     [head_norm.reshape(1, GLA_HV).astype(F32)], w_out, x, "gla_out")


HY_PAD = 128


def _hy_filter_kernel(bands_ref, w1t_ref, w1c_ref, w1s_ref, b1_ref, w2_ref, b2_ref, w3_ref, freq_ref, ld_ref,
                      h_ref, s_ref, *, seq_len, tile):
    i = pl.program_id(0)
    hp = lax.Precision.HIGHEST
    j = (i * tile + lax.broadcasted_iota(jnp.int32, (tile, 1), 0)).astype(F32)
    t = j / seq_len
    ang = (2.0 * math.pi / seq_len) * j * bands_ref[...]
    pre = (t * w1t_ref[...]
           + jnp.dot(jnp.cos(ang), w1c_ref[...], precision=hp, preferred_element_type=F32)
           + jnp.dot(jnp.sin(-ang), w1s_ref[...], precision=hp, preferred_element_type=F32)
           + b1_ref[...])
    h = jnp.sin(freq_ref[0:1, :] * pre)
    h = jnp.sin(freq_ref[1:2, :] * (jnp.dot(h, w2_ref[...], precision=hp, preferred_element_type=F32) + b2_ref[...]))
    h = _mm(h.astype(BF16), w3_ref[...])
    dist = jnp.abs(j - (seq_len // 2)) / (seq_len / 2)
    h = h * jnp.exp(-dist * jnp.exp(ld_ref[...]))
    h_ref[...] = h.astype(h_ref.dtype)

    @pl.when(i == 0)
    def _():
        s_ref[...] = jnp.zeros_like(s_ref)

    s_ref[...] += jnp.sum(jnp.abs(h), axis=0, keepdims=True)


def _hy_filter(seq_len, w1, b1, w2, b2, w3, freq, log_decay):
    hid = HY_FILTER_HID
    c2 = w3.shape[1]
    tile = _pick(seq_len, (512, 256))
    bands = jnp.zeros((1, HY_PAD), F32).at[0, :HY_BANDS].set(jnp.linspace(1e-4, HY_BANDS - 1, HY_BANDS, dtype=F32))
    w1 = w1.astype(F32)
    w1c = jnp.zeros((HY_PAD, hid), F32).at[:HY_BANDS].set(w1[1:1 + HY_BANDS])
    w1s = jnp.zeros((HY_PAD, hid), F32).at[:HY_BANDS].set(w1[1 + HY_BANDS:])
    full = lambda shape: pl.BlockSpec(shape, lambda i: (0, 0))
    return pl.pallas_call(
        functools.partial(_hy_filter_kernel, seq_len=seq_len, tile=tile),
        grid=(seq_len // tile,),
        in_specs=[full((1, HY_PAD)), full((1, hid)), full((HY_PAD, hid)), full((HY_PAD, hid)), full((1, hid)),
                  full((hid, hid)), full((1, hid)), full((hid, c2)), full((2, hid)), full((1, c2))],
        out_specs=[pl.BlockSpec((tile, c2), lambda i: (i, 0)), pl.BlockSpec((1, c2), lambda i: (0, 0))],
        out_shape=[jax.ShapeDtypeStruct((seq_len, c2), BF16), jax.ShapeDtypeStruct((1, c2), F32)],
        compiler_params=_cp("arbitrary"),
        name="hy_filter",
    )(bands, w1[0:1], w1c, w1s, b1.reshape(1, hid).astype(F32), w2.astype(F32), b2.reshape(1, hid).astype(F32),
      w3.astype(BF16), freq.astype(F32), log_decay.reshape(1, c2).astype(F32))


def _hy_pre_kernel(cur_ref, prev_ref, next_ref, w_ref, b_ref, o_ref, *, tile, n_tiles):
    i = pl.program_id(1)
    hl = BF16_SUBLANES
    prev = jnp.where(i == 0, 0.0, prev_ref[...].astype(F32))
    nxt = jnp.where(i == n_tiles - 1, 0.0, next_ref[...].astype(F32))
    xp = jnp.concatenate([prev, cur_ref[...].astype(F32), nxt], axis=0)
    y = b_ref[...]
    for k in range(3):
        s = hl + k - 1
        y = y + xp[s:s + tile] * w_ref[k:k + 1, :]
    o_ref[...] = y.astype(o_ref.dtype)


def _hy_pre(u, batch, seq_len, short_w, short_b):
    n, c3 = u.shape
    w = D_MODEL
    tile = _pick(seq_len, (512, 256))
    n_tiles = seq_len // tile
    hl = BF16_SUBLANES
    per_tile, per_seq = tile // hl, seq_len // hl
    return pl.pallas_call(
        functools.partial(_hy_pre_kernel, tile=tile, n_tiles=n_tiles),
        grid=(batch, n_tiles, c3 // w),
        in_specs=[pl.BlockSpec((tile, w), lambda b, i, c: (b * n_tiles + i, c)),
                  pl.BlockSpec((hl, w), lambda b, i, c: (jnp.maximum(b * per_seq + i * per_tile - 1, b * per_seq), c)),
                  pl.BlockSpec((hl, w), lambda b, i, c: (jnp.minimum(b * per_seq + (i + 1) * per_tile,
                                                                     (b + 1) * per_seq - 1), c)),
                  pl.BlockSpec((3, w), lambda b, i, c: (0, c)),
                  pl.BlockSpec((1, w), lambda b, i, c: (0, c))],
        out_specs=pl.BlockSpec((tile, w), lambda b, i, c: (b * n_tiles + i, c)),
        out_shape=jax.ShapeDtypeStruct((n, c3), BF16),
        compiler_params=_cp("parallel", "parallel", "parallel"),
        name="hy_pre",
    )(u, u, u, short_w.astype(F32), short_b.reshape(1, c3).astype(F32))


def _fft_plan(seq_len):
    n = 2 * seq_len
    n2 = 256 if n >= 32768 else 128
    n1 = n // n2
    assert n1 % 4 == 0 and n1 * n2 == n
    return n, n1, n2, n1 // 2


def _fft_tables(seq_len):
    n, n1, n2, ah = _fft_plan(seq_len)
    b = jnp.arange(n2, dtype=jnp.int32)
    ka = jnp.arange(n1, dtype=jnp.int32)
    a = jnp.arange(ah, dtype=jnp.int32)
    w = 2.0 * math.pi / n
    m = (ka[None, :, None] * (b[:, None, None] + n2 * a[None, None, :])) % n
    ang = w * m.astype(F32)
    ft = jnp.concatenate([jnp.cos(ang), -jnp.sin(ang)], axis=1).astype(BF16)
    ap = a + n1 // 4
    m = ((b[:, None, None] + n2 * ap[None, :, None]) * ka[None, None, :]) % n
    ang = w * m.astype(F32)
    fti = (jnp.concatenate([jnp.cos(ang), -jnp.sin(ang)], axis=2) / n).astype(BF16)
    ang = (2.0 * math.pi / n2) * ((b[:, None] * b[None, :]) % n2).astype(F32)
    fr, fi = jnp.cos(ang), -jnp.sin(ang)
    m2 = jnp.block([[fr, -fi], [fi, fr]]).astype(BF16)
    m2i = jnp.block([[fr, fi], [-fi, fr]]).astype(BF16)
    return ft, fti, m2, m2i


def _fft_first_kernel(z_ref, ft_ref, o_ref, *, tb):
    for i in range(tb):
        o_ref[0, i] = _mm(ft_ref[i], z_ref[0, i]).astype(o_ref.dtype)


def _fft_first(zb, ft):
    bsz, n2, ah, c = zb.shape
    rows = ft.shape[1]
    cb = 1024
    tb = _pick(n2, (16, 8))
    return pl.pallas_call(
        functools.partial(_fft_first_kernel, tb=tb),
        grid=(bsz, n2 // tb, c // cb),
        in_specs=[pl.BlockSpec((1, tb, ah, cb), lambda b, j, cc: (b, j, 0, cc)),
                  pl.BlockSpec((tb, rows, ah), lambda b, j, cc: (j, 0, 0))],
        out_specs=pl.BlockSpec((1, tb, rows, cb), lambda b, j, cc: (b, j, 0, cc)),
        out_shape=jax.ShapeDtypeStruct((bsz, n2, rows, c), BF16),
        compiler_params=_cp("parallel", "parallel", "parallel"),
        name="fft_first",
    )(zb, ft)


def _fft_spectrum_kernel(a_ref, s_ref, m_ref, o_ref, *, tk):
    for i in range(tk):
        o_ref[0, i] = (_mm(m_ref[...], a_ref[0, i]) / s_ref[...]).astype(o_ref.dtype)


def _fft_conv_kernel(a_ref, h_ref, m_ref, mi_ref, o_ref, *, tk, n2):
    for i in range(tk):
        x = _mm(m_ref[...], a_ref[0, i])
        xr, xi = x[:n2], x[n2:]
        hr, hi = h_ref[0, i, :n2].astype(F32), h_ref[0, i, n2:].astype(F32)
        p = jnp.concatenate([xr * hr - xi * hi, xr * hi + xi * hr], axis=0).astype(BF16)
        o_ref[0, i] = _mm(mi_ref[...], p).astype(o_ref.dtype)


def _fft_mid(ak, m2, m2i=None, spec=None, spec_col=0, scale=None):
    bsz, n1, r2, c = ak.shape
    cb = 1024
    tk = _pick(n1, (4, 2))
    grid = (bsz, n1 // tk, c // cb)
    a_spec = pl.BlockSpec((1, tk, r2, cb), lambda b, k, cc: (b, k, 0, cc))
    m_spec = pl.BlockSpec((r2, r2), lambda b, k, cc: (0, 0))
    if spec is None:
        return pl.pallas_call(
            functools.partial(_fft_spectrum_kernel, tk=tk),
            grid=grid,
            in_specs=[a_spec, pl.BlockSpec((1, cb), lambda b, k, cc: (0, cc)), m_spec],
            out_specs=a_spec,
            out_shape=jax.ShapeDtypeStruct(ak.shape, BF16),
            compiler_params=_cp("parallel", "parallel", "parallel"),
            name="fft_spectrum",
        )(ak, scale, m2)
    return pl.pallas_call(
        functools.partial(_fft_conv_kernel, tk=tk, n2=r2 // 2),
        grid=grid,
        in_specs=[a_spec, pl.BlockSpec((1, tk, r2, cb), lambda b, k, cc: (0, k, 0, spec_col + cc)), m_spec, m_spec],
        out_specs=a_spec,
        out_shape=jax.ShapeDtypeStruct(ak.shape, BF16),
        compiler_params=_cp("parallel", "parallel", "parallel"),
        name="fft_conv",
    )(ak, spec, m2, m2i)


def _fft_last_kernel(c_ref, fti_ref, o_ref, *, tb):
    for i in range(tb):
        o_ref[0, i] = _mm(fti_ref[i], c_ref[0, i]).astype(o_ref.dtype)


def _fft_last(cb_arr, fti):
    bsz, n2, rows, c = cb_arr.shape
    ah = fti.shape[1]
    cb = 1024
    tb = _pick(n2, (16, 8))
    return pl.pallas_call(
        functools.partial(_fft_last_kernel, tb=tb),
        grid=(bsz, n2 // tb, c // cb),
        in_specs=[pl.BlockSpec((1, tb, rows, cb), lambda b, j, cc: (b, j, 0, cc)),
                  pl.BlockSpec((tb, ah, rows), lambda b, j, cc: (j, 0, 0))],
        out_specs=pl.BlockSpec((1, tb, ah, cb), lambda b, j, cc: (b, j, 0, cc)),
        out_shape=jax.ShapeDtypeStruct((bsz, n2, ah, c), BF16),
        compiler_params=_cp("parallel", "parallel", "parallel"),
        name="fft_last",
    )(cb_arr, fti)


def _to_b_major(z, bsz, n2, ah):
    c = z.shape[-1]
    return z.reshape(bsz, ah, n2, c).transpose(0, 2, 1, 3)


def _swap_digits(arr, inner):
    bsz, p, r, c = arr.shape
    return arr.reshape(bsz, p, 2, inner, c).transpose(0, 3, 2, 1, 4).reshape(bsz, inner, 2 * p, c)


def _fft_forward(z, bsz, seq_len, ft):
    _, n1, n2, ah = _fft_plan(seq_len)
    a = _fft_first(_to_b_major(z, bsz, n2, ah), ft)
    return _swap_digits(a, n1)


def _long_conv(z, bsz, seq_len, tables, spec, order):
    ft, fti, m2, m2i = tables
    _, n1, n2, ah = _fft_plan(seq_len)
    ak = _fft_forward(z, bsz, seq_len, ft)
    ck = _fft_mid(ak, m2, m2i, spec=spec, spec_col=order * (D_MODEL // 1024))
    y = _fft_last(_swap_digits(ck, n2), fti)
    return y.transpose(0, 2, 1, 3).reshape(bsz * seq_len, z.shape[-1])


def _hy_gate_kernel(x1_ref, y_ref, v_ref, skip_ref, o_ref):
    v = v_ref[...].astype(F32)
    o_ref[...] = (x1_ref[...].astype(F32) * (y_ref[...].astype(F32) + skip_ref[...] * v)).astype(o_ref.dtype)


def _hy_gate(u3, y, skip0):
    n, w = y.shape
    tm = _pick(n, (1024, 512, 256))
    return pl.pallas_call(
        _hy_gate_kernel,
        grid=(n // tm,),
        in_specs=[pl.BlockSpec((tm, w), lambda i: (i, 1)), pl.BlockSpec((tm, w), lambda i: (i, 0)),
                  pl.BlockSpec((tm, w), lambda i: (i, 0)), pl.BlockSpec((1, w), lambda i: (0, 0))],
        out_specs=pl.BlockSpec((tm, w), lambda i: (i, 0)),
        out_shape=jax.ShapeDtypeStruct((n, w), BF16),
        compiler_params=_cp("parallel"),
        name="hy_gate",
    )(u3, y, u3, skip0.reshape(1, w).astype(F32))


def _hyena_mixer(x, batch, seq_len, g, w_in, short_w, short_b, fw1, fb1, fw2, fb2, fw3, ffreq, log_decay, skip,
                 w_out):
    d = D_MODEL
    tables = _fft_tables(seq_len)
    filt, l1 = _hy_filter(seq_len, fw1, fb1, fw2, fb2, fw3, ffreq, log_decay)
    fk = _fft_forward(filt, 1, seq_len, tables[0])
    spec = _fft_mid(fk, tables[2], scale=l1)
    u3 = _hy_pre(_norm_matmul(x, g, w_in), batch, seq_len, short_w, short_b)
    y1 = _long_conv(u3[:, :d], batch, seq_len, tables, spec, 0)
    z1 = _hy_gate(u3, y1, skip[0])
    y2 = _long_conv(z1, batch, seq_len, tables, spec, 1)

    def prologue(x2, yy, zz, sk):
        return x2.astype(F32) * (yy.astype(F32) + sk * zz.astype(F32))

    return _proj_res(prologue, [(u3, d, 2, None), (y2, d, 0, None), (z1, d, 0, None)],
                     [skip[1].reshape(1, d).astype(F32)], w_out, x, "hy_out")


def _trunk(x, mem, p):
    batch, seq_len, d = x.shape
    x = x.reshape(batch * seq_len, d)
    mem = mem.reshape(batch * MEM_TOKENS, d)
    depth = p["norm_mix"].shape[0]
    for i in range(depth):
        m, j = i % 4, i // 4
        g = p["norm_mix"][i]
        if m == 0:
            x = _na_mixer(x, batch, seq_len, g, p["na_w_qkv"][j], p["na_rpb"][j], p["na_w_out"][j])
        elif m == 1:
            x = _lru_mixer(x, batch, seq_len, g, p["lru_w_in"][j], p["lru_conv_w"][j], p["lru_conv_b"][j],
                           p["lru_gate_a_w"][j], p["lru_gate_a_b"][j], p["lru_gate_x_w"][j], p["lru_gate_x_b"][j],
                           p["lru_lambda"][j], p["lru_w_out"][j])
        elif m == 2:
            x = _gla_mixer(x, batch, seq_len, g, p["gla_w_in"][j], p["gla_gate_w1"][j], p["gla_gate_w2"][j],
                           p["gla_gate_b"][j], p["gla_head_norm"][j], p["gla_w_out"][j])
        else:
            x = _hyena_mixer(x, batch, seq_len, g, p["hy_w_in"][j], p["hy_short_w"][j], p["hy_short_b"][j],
                             p["hy_filt_w1"][j], p["hy_filt_b1"][j], p["hy_filt_w2"][j], p["hy_filt_b2"][j],
                             p["hy_filt_w3"][j], p["hy_filt_freq"][j], p["hy_log_decay"][j], p["hy_skip"][j],
                             p["hy_w_out"][j])
        kv = _norm_matmul(mem, p["norm_mem"][i], p["xa_w_kv"][i])
        x = _xattn(x, seq_len, p["norm_xattn"][i], p["xa_w_q"][i], kv, p["xa_w_out"][i])
        if i % 2 == 0:
            x = _ffn(x, p["norm_ffn"][i], p["ffn_w_gate_up"][i // 2], p["ffn_w_down"][i // 2])
        else:
            x = _moe(x, p["norm_ffn"][i], p["moe_router"][i // 2], p["moe_w_gate_up"][i // 2], p["moe_w_down"][i // 2])
    return _final_norm(x, p["norm_final"]).reshape(batch, seq_len, d)


_MATMUL_WEIGHTS = ("na_w_qkv", "na_w_out", "lru_w_in", "lru_w_out", "gla_w_in", "gla_w_out", "hy_w_in", "hy_w_out",
                   "xa_w_q", "xa_w_kv", "xa_w_out", "ffn_w_gate_up", "ffn_w_down", "moe_w_gate_up", "moe_w_down")


def kernel(x_prompt, x_sample, mem_prompt, mem_sample, norm_mix, norm_xattn, norm_mem, norm_ffn, norm_final, na_w_qkv, na_rpb, na_w_out, lru_w_in, lru_conv_w, lru_conv_b, lru_gate_a_w, lru_gate_a_b, lru_gate_x_w, lru_gate_x_b, lru_lambda, lru_w_out, gla_w_in, gla_gate_w1, gla_gate_w2, gla_gate_b, gla_head_norm, gla_w_out, hy_w_in, hy_short_w, hy_short_b, hy_filt_w1, hy_filt_b1, hy_filt_w2, hy_filt_b2, hy_filt_w3, hy_filt_freq, hy_log_decay, hy_skip, hy_w_out, xa_w_q, xa_w_kv, xa_w_out, ffn_w_gate_up, ffn_w_down, moe_router, moe_w_gate_up, moe_w_down):
    p = dict(norm_mix=norm_mix, norm_xattn=norm_xattn, norm_mem=norm_mem, norm_ffn=norm_ffn, norm_final=norm_final,
             na_w_qkv=na_w_qkv, na_rpb=na_rpb, na_w_out=na_w_out,
             lru_w_in=lru_w_in, lru_conv_w=lru_conv_w, lru_conv_b=lru_conv_b, lru_gate_a_w=lru_gate_a_w,
             lru_gate_a_b=lru_gate_a_b, lru_gate_x_w=lru_gate_x_w, lru_gate_x_b=lru_gate_x_b, lru_lambda=lru_lambda,
             lru_w_out=lru_w_out,
             gla_w_in=gla_w_in, gla_gate_w1=gla_gate_w1, gla_gate_w2=gla_gate_w2, gla_gate_b=gla_gate_b,
             gla_head_norm=gla_head_norm, gla_w_out=gla_w_out,
             hy_w_in=hy_w_in, hy_short_w=hy_short_w, hy_short_b=hy_short_b, hy_filt_w1=hy_filt_w1,
             hy_filt_b1=hy_filt_b1, hy_filt_w2=hy_filt_w2, hy_filt_b2=hy_filt_b2, hy_filt_w3=hy_filt_w3,
             hy_filt_freq=hy_filt_freq, hy_log_decay=hy_log_decay, hy_skip=hy_skip, hy_w_out=hy_w_out,
             xa_w_q=xa_w_q, xa_w_kv=xa_w_kv, xa_w_out=xa_w_out,
             ffn_w_gate_up=ffn_w_gate_up, ffn_w_down=ffn_w_down,
             moe_router=moe_router, moe_w_gate_up=moe_w_gate_up, moe_w_down=moe_w_down)
    for name in _MATMUL_WEIGHTS:
        p[name] = p[name].astype(BF16)
    return (_trunk(x_prompt, mem_prompt, p), _trunk(x_sample, mem_sample, p))
```

```python
import functools
import math

import jax
import jax.numpy as jnp
from jax import lax
from jax.experimental import pallas as pl
from jax.experimental.pallas import tpu as pltpu

F32 = jnp.float32
BF16 = jnp.bfloat16

D_MODEL = 1024
RMS_EPS = 1e-6
GRID_W = 64
NA_HEADS = 16
NA_HEAD_DIM = 64
NA_WIN_ROWS = 8
NA_WIN_COLS = 16
LRU_C = 8.0
LRU_CONV = 4
GLA_HEADS = 4
GLA_DK = 512
GLA_DV = 1024
GLA_HK = 128
GLA_HV = 256
GLA_GATE_RANK = 16
GLA_TAU = 16.0
GLA_CHUNK = 64
HY_BANDS = 16
HY_FILTER_HID = 64
MEM_TOKENS = 256
XA_HEADS = 4
XA_HEAD_DIM = 256
D_FF = 3584
N_EXPERTS = 8
NEG_BIG = -1e30

V7X_VMEM_BYTES = 64 * 1024 * 1024
VMEM_LIMIT = V7X_VMEM_BYTES - 8 * 1024 * 1024
BF16_SUBLANES = 16


def _cp(*sem):
    return pltpu.CompilerParams(dimension_semantics=sem, vmem_limit_bytes=VMEM_LIMIT)


def _pick(n, cands):
    for c in cands:
        if n % c == 0:
            return c
    return n


def _rms(x, g):
    return x * lax.rsqrt(jnp.mean(x * x, axis=-1, keepdims=True) + RMS_EPS) * g


def _sigmoid(x):
    return 1.0 / (1.0 + jnp.exp(-x))


def _softplus(x):
    return jnp.maximum(x, 0.0) + jnp.log1p(jnp.exp(-jnp.abs(x)))


def _mm(a, b):
    return jnp.dot(a, b, preferred_element_type=F32)


def _mm_nt(a, b):
    return lax.dot_general(a, b, (((1,), (1,)), ((), ())), preferred_element_type=F32)


def _mm_tn(a, b):
    return lax.dot_general(a, b, (((0,), (0,)), ((), ())), preferred_element_type=F32)


def _norm_matmul_kernel(x_ref, g_ref, w_ref, o_ref, *, tn):
    xn = _rms(x_ref[...], g_ref[...]).astype(BF16)
    for j in range(o_ref.shape[1] // tn):
        cols = slice(j * tn, (j + 1) * tn)
        o_ref[:, cols] = _mm(xn, w_ref[:, cols]).astype(o_ref.dtype)


def _norm_matmul(x, g, w, out_dtype=BF16):
    n, d = x.shape
    f = w.shape[1]
    tm = _pick(n, (1024, 512, 256))
    tn = _pick(f, (1024, 640, 512, 256, 128))
    return pl.pallas_call(
        functools.partial(_norm_matmul_kernel, tn=tn),
        grid=(n // tm,),
        in_specs=[pl.BlockSpec((tm, d), lambda i: (i, 0)),
                  pl.BlockSpec((1, d), lambda i: (0, 0)),
                  pl.BlockSpec((d, f), lambda i: (0, 0))],
        out_specs=pl.BlockSpec((tm, f), lambda i: (i, 0)),
        out_shape=jax.ShapeDtypeStruct((n, f), out_dtype),
        compiler_params=_cp("parallel"),
        name="norm_matmul",
    )(x, g.reshape(1, d), w)


def _final_norm_kernel(x_ref, g_ref, o_ref):
    o_ref[...] = _rms(x_ref[...], g_ref[...])


def _final_norm(x, g):
    n, d = x.shape
    tm = _pick(n, (1024, 512, 256))
    return pl.pallas_call(
        _final_norm_kernel,
        grid=(n // tm,),
        in_specs=[pl.BlockSpec((tm, d), lambda i: (i, 0)), pl.BlockSpec((1, d), lambda i: (0, 0))],
        out_specs=pl.BlockSpec((tm, d), lambda i: (i, 0)),
        out_shape=jax.ShapeDtypeStruct((n, d), F32),
        compiler_params=_cp("parallel"),
        name="final_norm",
    )(x, g.reshape(1, d))


def _proj_res(prologue, tok_inputs, row_inputs, w, res, name):
    n, d = res.shape
    k = w.shape[0]
    tm = _pick(n, (1024, 512, 256))
    n_tok, n_row = len(tok_inputs), len(row_inputs)

    def kern(*refs):
        toks = []
        for r, (_, _, _, lead) in zip(refs[:n_tok], tok_inputs):
            toks.append(r[0] if lead is not None else r[...])
        rows = [r[...] for r in refs[n_tok:n_tok + n_row]]
        w_ref, res_ref, o_ref = refs[n_tok + n_row:]
        a = prologue(*toks, *rows)
        o_ref[...] = res_ref[...] + _mm(a.astype(BF16), w_ref[...])

    in_specs, args = [], []
    for arr, width, col, lead in tok_inputs:
        if lead is None:
            in_specs.append(pl.BlockSpec((tm, width), lambda i, c=col: (i, c)))
        else:
            in_specs.append(pl.BlockSpec((1, tm, width), lambda i, c=col, l=lead: (l, i, c)))
        args.append(arr)
    for arr in row_inputs:
        in_specs.append(pl.BlockSpec(arr.shape, lambda i: (0, 0)))
        args.append(arr)
    in_specs += [pl.BlockSpec((k, d), lambda i: (0, 0)), pl.BlockSpec((tm, d), lambda i: (i, 0))]
    args += [w, res]
    return pl.pallas_call(
        kern,
        grid=(n // tm,),
        in_specs=in_specs,
        out_specs=pl.BlockSpec((tm, d), lambda i: (i, 0)),
        out_shape=jax.ShapeDtypeStruct((n, d), F32),
        compiler_params=_cp("parallel"),
        name=name,
    )(*args)


def _xattn_kernel(x_ref, g_ref, wq_ref, k_ref, v_ref, wo_ref, o_ref):
    x = x_ref[...]
    xn = _rms(x, g_ref[...]).astype(BF16)
    q = (_mm(xn, wq_ref[...]) * (XA_HEAD_DIM ** -0.5)).astype(BF16)
    outs = []
    for h in range(XA_HEADS):
        sl = slice(h * XA_HEAD_DIM, (h + 1) * XA_HEAD_DIM)
        s = _mm_nt(q[:, sl], k_ref[:, sl])
        p = jnp.exp(s - jnp.max(s, axis=-1, keepdims=True))
        p = p / jnp.sum(p, axis=-1, keepdims=True)
        outs.append(_mm(p.astype(BF16), v_ref[:, sl]))
    o = jnp.concatenate(outs, axis=-1).astype(BF16)
    o_ref[...] = x + _mm(o, wo_ref[...])


def _xattn(x, seq_len, g, wq, kv, wo):
    n, d = x.shape
    tm = _pick(seq_len, (512, 256))
    per_b = seq_len // tm
    m = MEM_TOKENS
    return pl.pallas_call(
        _xattn_kernel,
        grid=(n // tm,),
        in_specs=[pl.BlockSpec((tm, d), lambda i: (i, 0)),
                  pl.BlockSpec((1, d), lambda i: (0, 0)),
                  pl.BlockSpec((d, d), lambda i: (0, 0)),
                  pl.BlockSpec((m, d), lambda i: (i // per_b, 0)),
                  pl.BlockSpec((m, d), lambda i: (i // per_b, 1)),
                  pl.BlockSpec((d, d), lambda i: (0, 0))],
        out_specs=pl.BlockSpec((tm, d), lambda i: (i, 0)),
        out_shape=jax.ShapeDtypeStruct((n, d), F32),
        compiler_params=_cp("parallel"),
        name="xattn",
    )(x, g.reshape(1, d), wq, kv, kv, wo)


def _ffn_kernel(x_ref, g_ref, wg_ref, wu_ref, wd_ref, o_ref, xn_ref, acc_ref):
    f = pl.program_id(1)

    @pl.when(f == 0)
    def _():
        xn_ref[...] = _rms(x_ref[...], g_ref[...]).astype(BF16)
        acc_ref[...] = jnp.zeros_like(acc_ref)

    xn = xn_ref[...]
    hg = _mm(xn, wg_ref[...])
    hu = _mm(xn, wu_ref[...])
    a = (hg * _sigmoid(hg) * hu).astype(BF16)
    acc_ref[...] += _mm(a, wd_ref[...])

    @pl.when(f == pl.num_programs(1) - 1)
    def _():
        o_ref[...] = x_ref[...] + acc_ref[...]


def _ffn(x, g, w_gu, w_d):
    n, d = x.shape
    ff = w_d.shape[0]
    tm = _pick(n, (1024, 512, 256))
    tf = _pick(ff, (512, 256, 128))
    nf = ff // tf
    return pl.pallas_call(
        _ffn_kernel,
        grid=(n // tm, nf),
        in_specs=[pl.BlockSpec((tm, d), lambda i, f: (i, 0)),
                  pl.BlockSpec((1, d), lambda i, f: (0, 0)),
                  pl.BlockSpec((d, tf), lambda i, f: (0, f)),
                  pl.BlockSpec((d, tf), lambda i, f: (0, f + nf)),
                  pl.BlockSpec((tf, d), lambda i, f: (f, 0))],
        out_specs=pl.BlockSpec((tm, d), lambda i, f: (i, 0)),
        out_shape=jax.ShapeDtypeStruct((n, d), F32),
        scratch_shapes=[pltpu.VMEM((tm, d), BF16), pltpu.VMEM((tm, d), F32)],
        compiler_params=_cp("parallel", "arbitrary"),
        name="ffn",
    )(x, g.reshape(1, d), w_gu, w_gu, w_d)


def _router_kernel(x_ref, g_ref, wh_ref, wl_ref, idx_ref, wts_ref):
    xn = _rms(x_ref[...], g_ref[...])
    xh = xn.astype(BF16)
    xl = (xn - xh.astype(F32)).astype(BF16)
    logits = _mm(xh, wh_ref[...]) + _mm(xh, wl_ref[...]) + _mm(xl, wh_ref[...])
    lane = lax.broadcasted_iota(jnp.int32, logits.shape, 1)
    logits = jnp.where(lane < N_EXPERTS, logits, NEG_BIG)
    m1 = jnp.max(logits, axis=-1, keepdims=True)
    i1 = jnp.min(jnp.where(logits == m1, lane, 128), axis=-1, keepdims=True)
    rest = jnp.where(lane == i1, NEG_BIG, logits)
    m2 = jnp.max(rest, axis=-1, keepdims=True)
    i2 = jnp.min(jnp.where(rest == m2, lane, 128), axis=-1, keepdims=True)
    e2 = jnp.exp(m2 - m1)
    den = 1.0 + e2
    idx_ref[...] = jnp.where(lane == 0, i1, jnp.where(lane == 1, i2, 0))
    wts_ref[...] = jnp.where(lane == 0, 1.0 / den, jnp.where(lane == 1, e2 / den, 0.0))


def _router(x, g, w_router):
    n, d = x.shape
    tm = _pick(n, (1024, 512, 256))
    wp = jnp.zeros((d, 128), F32).at[:, :N_EXPERTS].set(w_router)
    wh = wp.astype(BF16)
    wl = (wp - wh.astype(F32)).astype(BF16)
    return pl.pallas_call(
        _router_kernel,
        grid=(n // tm,),
        in_specs=[pl.BlockSpec((tm, d), lambda i: (i, 0)),
                  pl.BlockSpec((1, d), lambda i: (0, 0)),
                  pl.BlockSpec((d, 128), lambda i: (0, 0)),
                  pl.BlockSpec((d, 128), lambda i: (0, 0))],
        out_specs=[pl.BlockSpec((tm, 128), lambda i: (i, 0)), pl.BlockSpec((tm, 128), lambda i: (i, 0))],
        out_shape=[jax.ShapeDtypeStruct((n, 128), jnp.int32), jax.ShapeDtypeStruct((n, 128), F32)],
        compiler_params=_cp("parallel"),
        name="router",
    )(x, g.reshape(1, d), wh, wl)


F32_SUBLANES = 8


def _row_copy(src_hbm, row, buf, sem, slot, grp, sub):
    return pltpu.make_async_copy(src_hbm.at[pl.ds(row, 1), :], buf.at[slot, grp, pl.ds(sub, 1), :], sem.at[slot])


def _row_gather_groups(src_hbm, idx_ref, buf, sem, slot, first_grp, n_groups):
    for g in range(n_groups):
        for sub in range(F32_SUBLANES):
            row = idx_ref[(first_grp + g) * F32_SUBLANES + sub]
            _row_copy(src_hbm, row, buf, sem, slot, first_grp + g, sub).start()


def _row_gather_start(src_hbm, idx_ref, buf, sem, slot, n_rows):
    def body(g, carry):
        _row_gather_groups(src_hbm, idx_ref, buf, sem, slot, g, 1)
        return carry

    lax.fori_loop(0, n_rows // F32_SUBLANES, body, 0, unroll=2)


def _row_gather_wait(buf, sem, slot):
    pltpu.make_async_copy(buf.at[slot], buf.at[slot], sem.at[slot]).wait()


def _moe_expert_kernel(te_ref, na_ref, src_cur_ref, src_nxt_ref, x_hbm, g_ref, wg_ref, wu_ref, wd_ref, o_ref,
                       buf, xg_ref, acc_ref, sem, *, tm, nf):
    i = pl.program_id(0)
    f = pl.program_id(1)
    n_active = na_ref[0]
    active = i < n_active
    slot = i % 2
    d = xg_ref.shape[1]
    n_groups = tm // F32_SUBLANES
    groups_main = -(-n_groups // nf)
    groups_last = n_groups - groups_main * (nf - 1)
    assert nf >= 3 and groups_last > 0

    @pl.when((f == 0) & (i == 0))
    def _():
        _row_gather_start(x_hbm, src_cur_ref, buf, sem, 0, tm)

    @pl.when((f == 0) & (i <= n_active))
    def _():
        _row_gather_wait(buf, sem, slot)

    @pl.when((f == 0) & active)
    def _():
        xg_ref[...] = _rms(buf[slot].reshape(tm, d), g_ref[...]).astype(BF16)

    def step(n_issue, sink):
        _row_gather_groups(x_hbm, src_nxt_ref, buf, sem, 1 - slot, f * groups_main, n_issue)
        xn = xg_ref[...]
        hg = _mm(xn, wg_ref[0])
        hu = _mm(xn, wu_ref[0])
        a = (hg * _sigmoid(hg) * hu).astype(BF16)
        sink(_mm(a, wd_ref[0]))

    def set_acc(y):
        acc_ref[...] = y

    def add_acc(y):
        acc_ref[...] += y

    def emit(y):
        o_ref[...] = acc_ref[...] + y

    pl.when(active & (f == 0))(lambda: step(groups_main, set_acc))
    pl.when(active & (f > 0) & (f < nf - 1))(lambda: step(groups_main, add_acc))
    pl.when(active & (f == nf - 1))(lambda: step(groups_last, emit))

    @pl.when((f == nf - 1) & jnp.logical_not(active))
    def _():
        o_ref[...] = jnp.zeros_like(o_ref)


def _moe_combine_kernel(pos_cur_ref, pos_nxt_ref, x_ref, w_ref, y_hbm, o_ref, buf, sem, *, tc):
    i = pl.program_id(0)
    slot = i % 2

    @pl.when(i == 0)
    def _():
        _row_gather_start(y_hbm, pos_cur_ref, buf, sem, 0, 2 * tc)

    _row_gather_wait(buf, sem, slot)

    @pl.when(i + 1 < pl.num_programs(0))
    def _():
        _row_gather_start(y_hbm, pos_nxt_ref, buf, sem, 1 - slot, 2 * tc)

    w = w_ref[...]
    d = x_ref.shape[1]
    half = tc // F32_SUBLANES
    y0 = buf[slot, 0:half].reshape(tc, d)
    y1 = buf[slot, half:2 * half].reshape(tc, d)
    o_ref[...] = x_ref[...] + w[:, 0:1] * y0 + w[:, 1:2] * y1


MOE_ROW_TILE = 1024
MOE_COMBINE_TILE = 512


def _moe_plan(e_idx, tm, tc):
    n = e_idx.shape[0]
    n_assign = 2 * n
    max_tiles = n_assign // tm + N_EXPERTS
    e_flat = e_idx.reshape(n_assign)
    onehot = (e_flat[:, None] == jnp.arange(N_EXPERTS, dtype=jnp.int32)[None, :]).astype(jnp.int32)
    csum = jnp.cumsum(onehot, axis=0)
    rank = jnp.sum(onehot * csum, axis=1) - 1
    counts = csum[-1]
    padded = ((counts + tm - 1) // tm) * tm
    gend = jnp.cumsum(padded)
    pos = (gend - padded)[e_flat] + rank
    n_active = (gend[-1] // tm).astype(jnp.int32)
    tile_start = jnp.arange(max_tiles, dtype=jnp.int32) * tm
    tile_expert = jnp.sum((tile_start[:, None] >= gend[None, :]).astype(jnp.int32), axis=1)
    last_expert = jnp.sum((tile_start[jnp.maximum(n_active - 1, 0)] >= gend).astype(jnp.int32))
    tile_expert = jnp.where(tile_start < gend[-1], tile_expert, last_expert).astype(jnp.int32)
    src = jnp.zeros((max_tiles * tm,), jnp.int32).at[pos].set(
        jnp.arange(n_assign, dtype=jnp.int32) // 2, unique_indices=True, mode="promise_in_bounds")
    pos_tiles = pos.reshape(n // tc, tc, 2).transpose(0, 2, 1).reshape(n_assign)
    return src, tile_expert, n_active.reshape(1), pos_tiles.astype(jnp.int32), max_tiles


def _moe(x, g, w_router, w_gu, w_d):
    n, d = x.shape
    ff = w_d.shape[1]
    tm = min(MOE_ROW_TILE, n)
    tc = min(MOE_COMBINE_TILE, n // 2)
    tf = _pick(ff, (512, 256, 128))
    nf = ff // tf
    assert (2 * n) % tm == 0 and tm % F32_SUBLANES == 0
    idx, wts = _router(x, g, w_router)
    src, tile_expert, n_active, pos_tiles, max_tiles = _moe_plan(idx[:, :2], tm, tc)

    def wmap(col_off):
        def f(i, f_, te, na):
            return (te[i], 0, jnp.where(i < na[0], f_, nf - 1) + col_off)
        return f

    smem = functools.partial(pl.BlockSpec, memory_space=pltpu.SMEM)
    y = pl.pallas_call(
        functools.partial(_moe_expert_kernel, tm=tm, nf=nf),
        grid_spec=pltpu.PrefetchScalarGridSpec(
            num_scalar_prefetch=2,
            grid=(max_tiles, nf),
            in_specs=[smem((tm,), lambda i, f_, te, na: (i,)),
                      smem((tm,), lambda i, f_, te, na: (jnp.minimum(i + 1, max_tiles - 1),)),
                      pl.BlockSpec(memory_space=pl.ANY),
                      pl.BlockSpec((1, d), lambda i, f_, te, na: (0, 0)),
                      pl.BlockSpec((1, d, tf), wmap(0)),
                      pl.BlockSpec((1, d, tf), wmap(nf)),
                      pl.BlockSpec((1, tf, d), lambda i, f_, te, na: (te[i], jnp.where(i < na[0], f_, nf - 1), 0))],
            out_specs=pl.BlockSpec((tm, d), lambda i, f_, te, na: (i, 0)),
            scratch_shapes=[pltpu.VMEM((2, tm // F32_SUBLANES, F32_SUBLANES, d), F32), pltpu.VMEM((tm, d), BF16),
                            pltpu.VMEM((tm, d), F32), pltpu.SemaphoreType.DMA((2,))]),
        out_shape=jax.ShapeDtypeStruct((max_tiles * tm, d), F32),
        compiler_params=_cp("arbitrary", "arbitrary"),
        name="moe_experts",
    )(tile_expert, n_active, src, src, x, g.reshape(1, d), w_gu, w_gu, w_d)

    n_steps = n // tc
    return pl.pallas_call(
        functools.partial(_moe_combine_kernel, tc=tc),
        grid=(n_steps,),
        in_specs=[smem((2 * tc,), lambda i: (i,)),
                  smem((2 * tc,), lambda i: (jnp.minimum(i + 1, n_steps - 1),)),
                  pl.BlockSpec((tc, d), lambda i: (i, 0)),
                  pl.BlockSpec((tc, 128), lambda i: (i, 0)),
                  pl.BlockSpec(memory_space=pl.ANY)],
        out_specs=pl.BlockSpec((tc, d), lambda i: (i, 0)),
        out_shape=jax.ShapeDtypeStruct((n, d), F32),
        scratch_shapes=[pltpu.VMEM((2, 2 * tc // F32_SUBLANES, F32_SUBLANES, d), F32), pltpu.SemaphoreType.DMA((2,))],
        compiler_params=_cp("arbitrary"),
        name="moe_combine",
    )(pos_tiles, pos_tiles, x, wts, y)


NA_GROUP = 4
NA_KEYS = NA_WIN_ROWS * GRID_W
NA_BLK = 8 * GRID_W


def _na_bias_table(rpb):
    c = jnp.arange(GRID_W)
    cs = jnp.clip(c - NA_WIN_COLS // 2, 0, GRID_W - NA_WIN_COLS)
    cp = jnp.arange(GRID_W)
    valid = (cp[None, :] >= cs[:, None]) & (cp[None, :] < cs[:, None] + NA_WIN_COLS)
    colrel = jnp.clip(cp[None, :] - c[:, None] + NA_WIN_COLS - 1, 0, 2 * NA_WIN_COLS - 2)
    rr = jnp.arange(NA_WIN_ROWS)[:, None] + jnp.arange(NA_WIN_ROWS)[None, :]
    tbl = rpb.astype(F32)[:, rr][:, :, :, colrel]
    tbl = jnp.where(valid[None, None, None], tbl, NEG_BIG)
    tbl = tbl.reshape(NA_HEADS // NA_GROUP, NA_GROUP, NA_WIN_ROWS, NA_WIN_ROWS, GRID_W, GRID_W)
    tbl = tbl.transpose(2, 0, 3, 5, 1, 4)
    return tbl.reshape(NA_WIN_ROWS, NA_HEADS // NA_GROUP, NA_KEYS, NA_GROUP * GRID_W)


def _na_kernel(q_ref, kp_ref, kc_ref, kn_ref, vp_ref, vc_ref, vn_ref, bias_ref, o_ref, kst, vst, *, rows):
    r = pl.program_id(0) % rows
    i = r // 8

    @pl.when(r % 8 == 0)
    def _():
        kst[0:NA_BLK] = kp_ref[...]
        kst[NA_BLK:2 * NA_BLK] = kc_ref[...]
        kst[2 * NA_BLK:3 * NA_BLK] = kn_ref[...]
        vst[0:NA_BLK] = vp_ref[...]
        vst[NA_BLK:2 * NA_BLK] = vc_ref[...]
        vst[2 * NA_BLK:3 * NA_BLK] = vn_ref[...]

    rs = jnp.clip(r - NA_WIN_ROWS // 2, 0, rows - NA_WIN_ROWS)
    off = pl.multiple_of((rs - 8 * (i - 1)) * GRID_W, GRID_W)
    gw = NA_GROUP * NA_HEAD_DIM
    lane_head = lax.broadcasted_iota(jnp.int32, (GRID_W, gw), 1) // NA_HEAD_DIM
    for grp in range(NA_HEADS // NA_GROUP):
        sl = slice(grp * gw, (grp + 1) * gw)
        qg = q_ref[:, sl] * (NA_HEAD_DIM ** -0.5)
        bq = jnp.concatenate([jnp.where(lane_head == h, qg, jnp.zeros_like(qg)) for h in range(NA_GROUP)], axis=0)
        kw = kst[pl.ds(off, NA_KEYS), sl]
        st = _mm_nt(kw, bq) + bias_ref[0, grp]
        p = jnp.exp(st - jnp.max(st, axis=0, keepdims=True))
        p = (p / jnp.sum(p, axis=0, keepdims=True)).astype(BF16)
        vw = vst[pl.ds(off, NA_KEYS), sl]
        res = _mm_tn(p, vw)
        o = jnp.zeros((GRID_W, gw), F32)
        for h in range(NA_GROUP):
            o = o + jnp.where(lane_head == h, res[h * GRID_W:(h + 1) * GRID_W], 0.0)
        o_ref[:, sl] = o.astype(o_ref.dtype)


def _na_attention(qkv, batch, seq_len, bias_tbl):
    n = qkv.shape[0]
    d = D_MODEL
    rows = seq_len // GRID_W
    assert rows % 8 == 0 and rows >= 16
    nb = rows // 8

    def kmap(delta, col):
        def f(g):
            b = g // rows
            i = (g % rows) // 8
            return (b * nb + jnp.clip(i + delta, 0, nb - 1), col)
        return f

    def bias_map(g):
        r = g % rows
        return (jnp.clip(r - NA_WIN_ROWS // 2, 0, rows - NA_WIN_ROWS) - r + NA_WIN_ROWS - 1, 0, 0, 0)

    kv_specs = [pl.BlockSpec((NA_BLK, d), kmap(dl, col)) for col in (1, 2) for dl in (-1, 0, 1)]
    return pl.pallas_call(
        functools.partial(_na_kernel, rows=rows),
        grid=(batch * rows,),
        in_specs=[pl.BlockSpec((GRID_W, d), lambda g: (g, 0))] + kv_specs
                 + [pl.BlockSpec((1,) + bias_tbl.shape[1:], bias_map)],
        out_specs=pl.BlockSpec((GRID_W, d), lambda g: (g, 0)),
        out_shape=jax.ShapeDtypeStruct((n, d), BF16),
        scratch_shapes=[pltpu.VMEM((3 * NA_BLK, d), BF16), pltpu.VMEM((3 * NA_BLK, d), BF16)],
        compiler_params=_cp("arbitrary"),
        name="na_attention",
    )(qkv, qkv, qkv, qkv, qkv, qkv, qkv, bias_tbl)


def _na_mixer(x, batch, seq_len, g, w_qkv, rpb, w_out):
    qkv = _norm_matmul(x, g, w_qkv)
    o = _na_attention(qkv, batch, seq_len, _na_bias_table(rpb))
    return _proj_res(lambda a: a, [(o, D_MODEL, 0, None)], [], w_out, x, "na_out")


LRU_LANES = 128


def _halo_maps(seq_len, tile, halo, col, tile_of):
    per_tile = tile // halo
    per_seq = seq_len // halo

    def prev(*g):
        return (jnp.maximum(g[0] * per_seq + tile_of(*g) * per_tile - 1, g[0] * per_seq), col)

    def nxt(*g):
        return (jnp.minimum(g[0] * per_seq + (tile_of(*g) + 1) * per_tile, (g[0] + 1) * per_seq - 1), col)

    return prev, nxt


def _lru_kernel(cur_ref, prev_ref, next_ref, cw_ref, cb_ref, wg_ref, ba_ref, bx_ref, lam_ref, o_ref,
                a_scr, b_scr, h_scr, carry, *, tile, n_tiles):
    d = pl.program_id(1)
    i = pl.program_id(2)
    ti = jnp.where(d == 0, i, n_tiles - 1 - i)
    hl = BF16_SUBLANES
    prev = jnp.where(ti == 0, 0.0, prev_ref[...].astype(F32))
    nxt = jnp.where(ti == n_tiles - 1, 0.0, next_ref[...].astype(F32))
    xp = jnp.concatenate([prev, cur_ref[...].astype(F32), nxt], axis=0)
    left = LRU_CONV // 2
    xc = cb_ref[...]
    for k in range(LRU_CONV):
        s = hl + k - left
        xc = xc + xp[s:s + tile] * cw_ref[k:k + 1, :]
    xcb = xc.astype(BF16)
    ga, gx = [], []
    for grp in range(D_MODEL // LRU_LANES):
        gg = _mm(xcb[:, grp * LRU_LANES:(grp + 1) * LRU_LANES], wg_ref[0, grp])
        ga.append(gg[:, :LRU_LANES])
        gx.append(gg[:, LRU_LANES:])
    ga = jnp.concatenate(ga, axis=-1) + ba_ref[0]
    gx = jnp.concatenate(gx, axis=-1) + bx_ref[0]
    log_a = -LRU_C * _sigmoid(ga) * _softplus(-lam_ref[0])
    a = jnp.exp(log_a)
    a_scr[...] = a
    b_scr[...] = jnp.sqrt(1.0 - a * a) * (_sigmoid(gx) * xc)

    @pl.when(i == 0)
    def _():
        carry[...] = jnp.zeros_like(carry)

    def body(t, h):
        tt = jnp.where(d == 0, t, tile - 1 - t)
        h = a_scr[pl.ds(tt, 1), :] * h + b_scr[pl.ds(tt, 1), :]
        h_scr[pl.ds(tt, 1), :] = h
        return h

    carry[...] = lax.fori_loop(0, tile, body, carry[...], unroll=8)
    o_ref[0] = h_scr[...].astype(o_ref.dtype)


def _lru_gate_weights(ga_w, gx_w):
    def bd(w):
        w = w.reshape(2, D_MODEL // LRU_LANES, 2, 64, 64)
        z = jnp.zeros_like(w[:, :, 0])
        top = jnp.concatenate([w[:, :, 0], z], axis=-1)
        bot = jnp.concatenate([z, w[:, :, 1]], axis=-1)
        return jnp.concatenate([top, bot], axis=-2)
    return jnp.concatenate([bd(ga_w), bd(gx_w)], axis=-1).astype(BF16)


def _lru_scan(gb, batch, seq_len, conv_w, conv_b, wg, ba, bx, lam):
    n = gb.shape[0]
    w = D_MODEL
    tile = _pick(seq_len, (512, 256))
    n_tiles = seq_len // tile
    hl = BF16_SUBLANES

    def tile_of(b, d, i):
        return jnp.where(d == 0, i, n_tiles - 1 - i)

    prev_map, next_map = _halo_maps(seq_len, tile, hl, 1, tile_of)
    row = lambda b, d, i: (d, 0, 0)
    return pl.pallas_call(
        functools.partial(_lru_kernel, tile=tile, n_tiles=n_tiles),
        grid=(batch, 2, n_tiles),
        in_specs=[pl.BlockSpec((tile, w), lambda b, d, i: (b * n_tiles + tile_of(b, d, i), 1)),
                  pl.BlockSpec((hl, w), prev_map),
                  pl.BlockSpec((hl, w), next_map),
                  pl.BlockSpec((LRU_CONV, w), lambda b, d, i: (0, 0)),
                  pl.BlockSpec((1, w), lambda b, d, i: (0, 0)),
                  pl.BlockSpec((1,) + wg.shape[1:], lambda b, d, i: (d, 0, 0, 0)),
                  pl.BlockSpec((1, 1, w), row), pl.BlockSpec((1, 1, w), row), pl.BlockSpec((1, 1, w), row)],
        out_specs=pl.BlockSpec((1, tile, w), lambda b, d, i: (d, b * n_tiles + tile_of(b, d, i), 0)),
        out_shape=jax.ShapeDtypeStruct((2, n, w), BF16),
        scratch_shapes=[pltpu.VMEM((tile, w), F32), pltpu.VMEM((tile, w), F32), pltpu.VMEM((tile, w), F32),
                        pltpu.VMEM((1, w), F32)],
        compiler_params=_cp("parallel", "arbitrary", "arbitrary"),
        name="lru_scan",
    )(gb, gb, gb, conv_w, conv_b.reshape(1, w), wg, ba.reshape(2, 1, w), bx.reshape(2, 1, w), lam.reshape(2, 1, w))


def _gelu_tanh(x):
    return 0.5 * x * (1.0 + jnp.tanh(math.sqrt(2.0 / math.pi) * (x + 0.044715 * (x * x * x))))


def _lru_mixer(x, batch, seq_len, g, w_in, conv_w, conv_b, ga_w, ga_b, gx_w, gx_b, lam, w_out):
    gb = _norm_matmul(x, g, w_in)
    h = _lru_scan(gb, batch, seq_len, conv_w, conv_b, _lru_gate_weights(ga_w, gx_w), ga_b, gx_b, lam)

    def prologue(gate, hf, hb):
        return _gelu_tanh(gate.astype(F32)) * (hf.astype(F32) + hb.astype(F32))

    return _proj_res(prologue, [(gb, D_MODEL, 0, None), (h, D_MODEL, 0, 0), (h, D_MODEL, 0, 1)], [], w_out, x,
                     "lru_out")


GLA_EXTRA = 128


def _gla_kernel(q_ref, k_ref, v_ref, lr_ref, w2_ref, gb_ref, o_ref, qb_scr, kb_scr, kd_scr, dec_scr, st_scr,
                *, tile, n_tiles):
    d = pl.program_id(1)
    i = pl.program_id(2)
    c = GLA_CHUNK
    n_chunks = tile // c

    @pl.when(i == 0)
    def _():
        st_scr[...] = jnp.zeros_like(st_scr)

    z = _mm(lr_ref[...], w2_ref[0]) + gb_ref[0]
    g = -_softplus(-z) / GLA_TAU
    row = lax.broadcasted_iota(jnp.int32, (tile, tile), 0)
    col = lax.broadcasted_iota(jnp.int32, (tile, tile), 1)
    same = (row // c) == (col // c)
    fwd = d == 0
    tri = same & (jnp.where(fwd, col, row) <= jnp.where(fwd, row, col))
    tri = jnp.where(tri, 1.0, 0.0).astype(BF16)
    ones = jnp.where(same, 1.0, 0.0).astype(BF16)
    g_hi = g.astype(BF16)
    g_lo = (g - g_hi.astype(F32)).astype(BF16)
    bsum = _mm(tri, g_hi) + _mm(tri, g_lo)
    tot = _mm(ones, g_hi) + _mm(ones, g_lo)
    q = q_ref[...].astype(F32) * (GLA_HK ** -0.5)
    k = k_ref[...].astype(F32)
    qb_scr[...] = (q * jnp.exp(bsum)).astype(BF16)
    kb_scr[...] = (k * jnp.exp(-bsum)).astype(BF16)
    kd_scr[...] = (k * jnp.exp(tot - bsum)).astype(BF16)
    dec_scr[...] = jnp.exp(tot)

    def chunk_pass(forward):
        keep = same & ((col <= row) if forward else (col > row))
        order = range(n_chunks) if forward else range(n_chunks - 1, -1, -1)
        for h in range(GLA_HEADS):
            ks = slice(h * GLA_HK, (h + 1) * GLA_HK)
            vs = slice(h * GLA_HV, (h + 1) * GLA_HV)
            qb = qb_scr[:, ks]
            vh = v_ref[:, vs]
            att = jnp.where(keep, _mm_nt(qb, kb_scr[:, ks]), 0.0).astype(BF16)
            o_intra = _mm(att, vh)
            s_t = st_scr[h]
            for ci in order:
                rows = slice(ci * c, (ci + 1) * c)
                o = o_intra[rows] + _mm_nt(qb[rows], s_t.astype(BF16))
                o_ref[0, rows, vs] = o.astype(o_ref.dtype)
                s_t = s_t * dec_scr[ci * c:ci * c + 1, ks] + _mm_tn(vh[rows], kd_scr[rows, ks])
            st_scr[h] = s_t

    @pl.when(fwd)
    def _():
        chunk_pass(True)

    @pl.when(jnp.logical_not(fwd))
    def _():
        chunk_pass(False)


def _gla_scan(proj, batch, seq_len, w2p, gate_b):
    n = proj.shape[0]
    tile = _pick(seq_len, (256,))
    n_tiles = seq_len // tile

    def tok(b, d, i):
        return b * n_tiles + jnp.where(d == 0, i, n_tiles - 1 - i)

    return pl.pallas_call(
        functools.partial(_gla_kernel, tile=tile, n_tiles=n_tiles),
        grid=(batch, 2, n_tiles),
        in_specs=[pl.BlockSpec((tile, GLA_DK), lambda b, d, i: (tok(b, d, i), 0)),
                  pl.BlockSpec((tile, GLA_DK), lambda b, d, i: (tok(b, d, i), 1)),
                  pl.BlockSpec((tile, GLA_DV), lambda b, d, i: (tok(b, d, i), 1)),
                  pl.BlockSpec((tile, GLA_EXTRA), lambda b, d, i: (tok(b, d, i), (2 * GLA_DK + 2 * GLA_DV) // GLA_EXTRA)),
                  pl.BlockSpec((1, GLA_EXTRA, GLA_DK), lambda b, d, i: (d, 0, 0)),
                  pl.BlockSpec((1, 1, GLA_DK), lambda b, d, i: (d, 0, 0))],
        out_specs=pl.BlockSpec((1, tile, GLA_DV), lambda b, d, i: (d, tok(b, d, i), 0)),
        out_shape=jax.ShapeDtypeStruct((2, n, GLA_DV), BF16),
        scratch_shapes=[pltpu.VMEM((tile, GLA_DK), BF16), pltpu.VMEM((tile, GLA_DK), BF16),
                        pltpu.VMEM((tile, GLA_DK), BF16), pltpu.VMEM((tile, GLA_DK), F32),
                        pltpu.VMEM((GLA_HEADS, GLA_HV, GLA_HK), F32)],
        compiler_params=_cp("parallel", "arbitrary", "arbitrary"),
        name="gla_scan",
    )(proj, proj, proj, proj, w2p, gate_b.reshape(2, 1, GLA_DK))


def _gla_mixer(x, batch, seq_len, g, w_in, gate_w1, gate_w2, gate_b, head_norm, w_out):
    d = D_MODEL
    pad = jnp.zeros((d, GLA_EXTRA - 2 * GLA_GATE_RANK), w_in.dtype)
    w_ext = jnp.concatenate([w_in, gate_w1[0].astype(w_in.dtype), gate_w1[1].astype(w_in.dtype), pad], axis=1)
    proj = _norm_matmul(x, g, w_ext)
    w2p = jnp.zeros((2, GLA_EXTRA, GLA_DK), F32)
    w2p = w2p.at[0, :GLA_GATE_RANK].set(gate_w2[0]).at[1, GLA_GATE_RANK:2 * GLA_GATE_RANK].set(gate_w2[1])
    o = _gla_scan(proj, batch, seq_len, w2p.astype(BF16), gate_b)

    def prologue(of, ob, r, hn):
        o_sum = of.astype(F32) + ob.astype(F32)
        parts = []
        for h in range(GLA_HEADS):
            oh = o_sum[:, h * GLA_HV:(h + 1) * GLA_HV]
            parts.append(_rms(oh, hn))
        r = r.astype(F32)
        return jnp.concatenate(parts, axis=-1) * (r * _sigmoid(r))

    return _proj_res(prologue, [(o, GLA_DV, 0, 0), (o, GLA_DV, 0, 1), (proj, GLA_DV, 2, None)],
                     [head_norm.reshape(1, GLA_HV).astype(F32)], w_out, x, "gla_out")


HY_PAD = 128


def _hy_filter_kernel(bands_ref, w1t_ref, w1c_ref, w1s_ref, b1_ref, w2_ref, b2_ref, w3_ref, freq_ref, ld_ref,
                      h_ref, s_ref, *, seq_len, tile):
    i = pl.program_id(0)
    hp = lax.Precision.HIGHEST
    j = (i * tile + lax.broadcasted_iota(jnp.int32, (tile, 1), 0)).astype(F32)
    t = j / seq_len
    ang = (2.0 * math.pi / seq_len) * j * bands_ref[...]
    pre = (t * w1t_ref[...]
           + jnp.dot(jnp.cos(ang), w1c_ref[...], precision=hp, preferred_element_type=F32)
           + jnp.dot(jnp.sin(-ang), w1s_ref[...], precision=hp, preferred_element_type=F32)
           + b1_ref[...])
    h = jnp.sin(freq_ref[0:1, :] * pre)
    h = jnp.sin(freq_ref[1:2, :] * (jnp.dot(h, w2_ref[...], precision=hp, preferred_element_type=F32) + b2_ref[...]))
    h = _mm(h.astype(BF16), w3_ref[...])
    dist = jnp.abs(j - (seq_len // 2)) / (seq_len / 2)
    h = h * jnp.exp(-dist * jnp.exp(ld_ref[...]))
    h_ref[...] = h.astype(h_ref.dtype)

    @pl.when(i == 0)
    def _():
        s_ref[...] = jnp.zeros_like(s_ref)

    s_ref[...] += jnp.sum(jnp.abs(h), axis=0, keepdims=True)


def _hy_filter(seq_len, w1, b1, w2, b2, w3, freq, log_decay):
    hid = HY_FILTER_HID
    c2 = w3.shape[1]
    tile = _pick(seq_len, (512, 256))
    bands = jnp.zeros((1, HY_PAD), F32).at[0, :HY_BANDS].set(jnp.linspace(1e-4, HY_BANDS - 1, HY_BANDS, dtype=F32))
    w1 = w1.astype(F32)
    w1c = jnp.zeros((HY_PAD, hid), F32).at[:HY_BANDS].set(w1[1:1 + HY_BANDS])
    w1s = jnp.zeros((HY_PAD, hid), F32).at[:HY_BANDS].set(w1[1 + HY_BANDS:])
    full = lambda shape: pl.BlockSpec(shape, lambda i: (0, 0))
    return pl.pallas_call(
        functools.partial(_hy_filter_kernel, seq_len=seq_len, tile=tile),
        grid=(seq_len // tile,),
        in_specs=[full((1, HY_PAD)), full((1, hid)), full((HY_PAD, hid)), full((HY_PAD, hid)), full((1, hid)),
                  full((hid, hid)), full((1, hid)), full((hid, c2)), full((2, hid)), full((1, c2))],
        out_specs=[pl.BlockSpec((tile, c2), lambda i: (i, 0)), pl.BlockSpec((1, c2), lambda i: (0, 0))],
        out_shape=[jax.ShapeDtypeStruct((seq_len, c2), BF16), jax.ShapeDtypeStruct((1, c2), F32)],
        compiler_params=_cp("arbitrary"),
        name="hy_filter",
    )(bands, w1[0:1], w1c, w1s, b1.reshape(1, hid).astype(F32), w2.astype(F32), b2.reshape(1, hid).astype(F32),
      w3.astype(BF16), freq.astype(F32), log_decay.reshape(1, c2).astype(F32))


def _hy_pre_kernel(cur_ref, prev_ref, next_ref, w_ref, b_ref, o_ref, *, tile, n_tiles):
    i = pl.program_id(1)
    hl = BF16_SUBLANES
    prev = jnp.where(i == 0, 0.0, prev_ref[...].astype(F32))
    nxt = jnp.where(i == n_tiles - 1, 0.0, next_ref[...].astype(F32))
    xp = jnp.concatenate([prev, cur_ref[...].astype(F32), nxt], axis=0)
    y = b_ref[...]
    for k in range(3):
        s = hl + k - 1
        y = y + xp[s:s + tile] * w_ref[k:k + 1, :]
    o_ref[...] = y.astype(o_ref.dtype)


def _hy_pre(u, batch, seq_len, short_w, short_b):
    n, c3 = u.shape
    w = D_MODEL
    tile = _pick(seq_len, (512, 256))
    n_tiles = seq_len // tile
    hl = BF16_SUBLANES
    per_tile, per_seq = tile // hl, seq_len // hl
    return pl.pallas_call(
        functools.partial(_hy_pre_kernel, tile=tile, n_tiles=n_tiles),
        grid=(batch, n_tiles, c3 // w),
        in_specs=[pl.BlockSpec((tile, w), lambda b, i, c: (b * n_tiles + i, c)),
                  pl.BlockSpec((hl, w), lambda b, i, c: (jnp.maximum(b * per_seq + i * per_tile - 1, b * per_seq), c)),
                  pl.BlockSpec((hl, w), lambda b, i, c: (jnp.minimum(b * per_seq + (i + 1) * per_tile,
                                                                     (b + 1) * per_seq - 1), c)),
                  pl.BlockSpec((3, w), lambda b, i, c: (0, c)),
                  pl.BlockSpec((1, w), lambda b, i, c: (0, c))],
        out_specs=pl.BlockSpec((tile, w), lambda b, i, c: (b * n_tiles + i, c)),
        out_shape=jax.ShapeDtypeStruct((n, c3), BF16),
        compiler_params=_cp("parallel", "parallel", "parallel"),
        name="hy_pre",
    )(u, u, u, short_w.astype(F32), short_b.reshape(1, c3).astype(F32))


def _fft_plan(seq_len):
    n = 2 * seq_len
    n2 = 256 if n >= 32768 else 128
    n1 = n // n2
    assert n1 % 4 == 0 and n1 * n2 == n
    return n, n1, n2, n1 // 2


def _fft_tables(seq_len):
    n, n1, n2, ah = _fft_plan(seq_len)
    b = jnp.arange(n2, dtype=jnp.int32)
    ka = jnp.arange(n1, dtype=jnp.int32)
    a = jnp.arange(ah, dtype=jnp.int32)
    w = 2.0 * math.pi / n
    m = (ka[None, :, None] * (b[:, None, None] + n2 * a[None, None, :])) % n
    ang = w * m.astype(F32)
    ft = jnp.concatenate([jnp.cos(ang), -jnp.sin(ang)], axis=1).astype(BF16)
    ap = a + n1 // 4
    m = ((b[:, None, None] + n2 * ap[None, :, None]) * ka[None, None, :]) % n
    ang = w * m.astype(F32)
    fti = (jnp.concatenate([jnp.cos(ang), -jnp.sin(ang)], axis=2) / n).astype(BF16)
    ang = (2.0 * math.pi / n2) * ((b[:, None] * b[None, :]) % n2).astype(F32)
    fr, fi = jnp.cos(ang), -jnp.sin(ang)
    m2 = jnp.block([[fr, -fi], [fi, fr]]).astype(BF16)
    m2i = jnp.block([[fr, fi], [-fi, fr]]).astype(BF16)
    return ft, fti, m2, m2i


def _fft_first_kernel(z_ref, ft_ref, o_ref, *, tb):
    for i in range(tb):
        o_ref[0, i] = _mm(ft_ref[i], z_ref[0, i]).astype(o_ref.dtype)


def _fft_first(zb, ft):
    bsz, n2, ah, c = zb.shape
    rows = ft.shape[1]
    cb = 1024
    tb = _pick(n2, (16, 8))
    return pl.pallas_call(
        functools.partial(_fft_first_kernel, tb=tb),
        grid=(bsz, n2 // tb, c // cb),
        in_specs=[pl.BlockSpec((1, tb, ah, cb), lambda b, j, cc: (b, j, 0, cc)),
                  pl.BlockSpec((tb, rows, ah), lambda b, j, cc: (j, 0, 0))],
        out_specs=pl.BlockSpec((1, tb, rows, cb), lambda b, j, cc: (b, j, 0, cc)),
        out_shape=jax.ShapeDtypeStruct((bsz, n2, rows, c), BF16),
        compiler_params=_cp("parallel", "parallel", "parallel"),
        name="fft_first",
    )(zb, ft)


def _fft_spectrum_kernel(a_ref, s_ref, m_ref, o_ref, *, tk):
    for i in range(tk):
        o_ref[0, i] = (_mm(m_ref[...], a_ref[0, i]) / s_ref[...]).astype(o_ref.dtype)


def _fft_conv_kernel(a_ref, h_ref, m_ref, mi_ref, o_ref, *, tk, n2):
    for i in range(tk):
        x = _mm(m_ref[...], a_ref[0, i])
        xr, xi = x[:n2], x[n2:]
        hr, hi = h_ref[0, i, :n2].astype(F32), h_ref[0, i, n2:].astype(F32)
        p = jnp.concatenate([xr * hr - xi * hi, xr * hi + xi * hr], axis=0).astype(BF16)
        o_ref[0, i] = _mm(mi_ref[...], p).astype(o_ref.dtype)


def _fft_mid(ak, m2, m2i=None, spec=None, spec_col=0, scale=None):
    bsz, n1, r2, c = ak.shape
    cb = 1024
    tk = _pick(n1, (4, 2))
    grid = (bsz, n1 // tk, c // cb)
    a_spec = pl.BlockSpec((1, tk, r2, cb), lambda b, k, cc: (b, k, 0, cc))
    m_spec = pl.BlockSpec((r2, r2), lambda b, k, cc: (0, 0))
    if spec is None:
        return pl.pallas_call(
            functools.partial(_fft_spectrum_kernel, tk=tk),
            grid=grid,
            in_specs=[a_spec, pl.BlockSpec((1, cb), lambda b, k, cc: (0, cc)), m_spec],
            out_specs=a_spec,
            out_shape=jax.ShapeDtypeStruct(ak.shape, BF16),
            compiler_params=_cp("parallel", "parallel", "parallel"),
            name="fft_spectrum",
        )(ak, scale, m2)
    return pl.pallas_call(
        functools.partial(_fft_conv_kernel, tk=tk, n2=r2 // 2),
        grid=grid,
        in_specs=[a_spec, pl.BlockSpec((1, tk, r2, cb), lambda b, k, cc: (0, k, 0, spec_col + cc)), m_spec, m_spec],
        out_specs=a_spec,
        out_shape=jax.ShapeDtypeStruct(ak.shape, BF16),
        compiler_params=_cp("parallel", "parallel", "parallel"),
        name="fft_conv",
    )(ak, spec, m2, m2i)


def _fft_last_kernel(c_ref, fti_ref, o_ref, *, tb):
    for i in range(tb):
        o_ref[0, i] = _mm(fti_ref[i], c_ref[0, i]).astype(o_ref.dtype)


def _fft_last(cb_arr, fti):
    bsz, n2, rows, c = cb_arr.shape
    ah = fti.shape[1]
    cb = 1024
    tb = _pick(n2, (16, 8))
    return pl.pallas_call(
        functools.partial(_fft_last_kernel, tb=tb),
        grid=(bsz, n2 // tb, c // cb),
        in_specs=[pl.BlockSpec((1, tb, rows, cb), lambda b, j, cc: (b, j, 0, cc)),
                  pl.BlockSpec((tb, ah, rows), lambda b, j, cc: (j, 0, 0))],
        out_specs=pl.BlockSpec((1, tb, ah, cb), lambda b, j, cc: (b, j, 0, cc)),
        out_shape=jax.ShapeDtypeStruct((bsz, n2, ah, c), BF16),
        compiler_params=_cp("parallel", "parallel", "parallel"),
        name="fft_last",
    )(cb_arr, fti)


def _to_b_major(z, bsz, n2, ah):
    c = z.shape[-1]
    return z.reshape(bsz, ah, n2, c).transpose(0, 2, 1, 3)


def _swap_digits(arr, inner):
    bsz, p, r, c = arr.shape
    return arr.reshape(bsz, p, 2, inner, c).transpose(0, 3, 2, 1, 4).reshape(bsz, inner, 2 * p, c)


def _fft_forward(z, bsz, seq_len, ft):
    _, n1, n2, ah = _fft_plan(seq_len)
    a = _fft_first(_to_b_major(z, bsz, n2, ah), ft)
    return _swap_digits(a, n1)


def _long_conv(z, bsz, seq_len, tables, spec, order):
    ft, fti, m2, m2i = tables
    _, n1, n2, ah = _fft_plan(seq_len)
    ak = _fft_forward(z, bsz, seq_len, ft)
    ck = _fft_mid(ak, m2, m2i, spec=spec, spec_col=order * (D_MODEL // 1024))
    y = _fft_last(_swap_digits(ck, n2), fti)
    return y.transpose(0, 2, 1, 3).reshape(bsz * seq_len, z.shape[-1])


def _hy_gate_kernel(x1_ref, y_ref, v_ref, skip_ref, o_ref):
    v = v_ref[...].astype(F32)
    o_ref[...] = (x1_ref[...].astype(F32) * (y_ref[...].astype(F32) + skip_ref[...] * v)).astype(o_ref.dtype)


def _hy_gate(u3, y, skip0):
    n, w = y.shape
    tm = _pick(n, (1024, 512, 256))
    return pl.pallas_call(
        _hy_gate_kernel,
        grid=(n // tm,),
        in_specs=[pl.BlockSpec((tm, w), lambda i: (i, 1)), pl.BlockSpec((tm, w), lambda i: (i, 0)),
                  pl.BlockSpec((tm, w), lambda i: (i, 0)), pl.BlockSpec((1, w), lambda i: (0, 0))],
        out_specs=pl.BlockSpec((tm, w), lambda i: (i, 0)),
        out_shape=jax.ShapeDtypeStruct((n, w), BF16),
        compiler_params=_cp("parallel"),
        name="hy_gate",
    )(u3, y, u3, skip0.reshape(1, w).astype(F32))


def _hyena_mixer(x, batch, seq_len, g, w_in, short_w, short_b, fw1, fb1, fw2, fb2, fw3, ffreq, log_decay, skip,
                 w_out):
    d = D_MODEL
    tables = _fft_tables(seq_len)
    filt, l1 = _hy_filter(seq_len, fw1, fb1, fw2, fb2, fw3, ffreq, log_decay)
    fk = _fft_forward(filt, 1, seq_len, tables[0])
    spec = _fft_mid(fk, tables[2], scale=l1)
    u3 = _hy_pre(_norm_matmul(x, g, w_in), batch, seq_len, short_w, short_b)
    y1 = _long_conv(u3[:, :d], batch, seq_len, tables, spec, 0)
    z1 = _hy_gate(u3, y1, skip[0])
    y2 = _long_conv(z1, batch, seq_len, tables, spec, 1)

    def prologue(x2, yy, zz, sk):
        return x2.astype(F32) * (yy.astype(F32) + sk * zz.astype(F32))

    return _proj_res(prologue, [(u3, d, 2, None), (y2, d, 0, None), (z1, d, 0, None)],
                     [skip[1].reshape(1, d).astype(F32)], w_out, x, "hy_out")


def _trunk(x, mem, p):
    batch, seq_len, d = x.shape
    x = x.reshape(batch * seq_len, d)
    mem = mem.reshape(batch * MEM_TOKENS, d)
    depth = p["norm_mix"].shape[0]
    for i in range(depth):
        m, j = i % 4, i // 4
        g = p["norm_mix"][i]
        if m == 0:
            x = _na_mixer(x, batch, seq_len, g, p["na_w_qkv"][j], p["na_rpb"][j], p["na_w_out"][j])
        elif m == 1:
            x = _lru_mixer(x, batch, seq_len, g, p["lru_w_in"][j], p["lru_conv_w"][j], p["lru_conv_b"][j],
                           p["lru_gate_a_w"][j], p["lru_gate_a_b"][j], p["lru_gate_x_w"][j], p["lru_gate_x_b"][j],
                           p["lru_lambda"][j], p["lru_w_out"][j])
        elif m == 2:
            x = _gla_mixer(x, batch, seq_len, g, p["gla_w_in"][j], p["gla_gate_w1"][j], p["gla_gate_w2"][j],
                           p["gla_gate_b"][j], p["gla_head_norm"][j], p["gla_w_out"][j])
        else:
            x = _hyena_mixer(x, batch, seq_len, g, p["hy_w_in"][j], p["hy_short_w"][j], p["hy_short_b"][j],
                             p["hy_filt_w1"][j], p["hy_filt_b1"][j], p["hy_filt_w2"][j], p["hy_filt_b2"][j],
                             p["hy_filt_w3"][j], p["hy_filt_freq"][j], p["hy_log_decay"][j], p["hy_skip"][j],
                             p["hy_w_out"][j])
        kv = _norm_matmul(mem, p["norm_mem"][i], p["xa_w_kv"][i])
        x = _xattn(x, seq_len, p["norm_xattn"][i], p["xa_w_q"][i], kv, p["xa_w_out"][i])
        if i % 2 == 0:
            x = _ffn(x, p["norm_ffn"][i], p["ffn_w_gate_up"][i // 2], p["ffn_w_down"][i // 2])
        else:
            x = _moe(x, p["norm_ffn"][i], p["moe_router"][i // 2], p["moe_w_gate_up"][i // 2], p["moe_w_down"][i // 2])
    return _final_norm(x, p["norm_final"]).reshape(batch, seq_len, d)


_MATMUL_WEIGHTS = ("na_w_qkv", "na_w_out", "lru_w_in", "lru_w_out", "gla_w_in", "gla_w_out", "hy_w_in", "hy_w_out",
                   "xa_w_q", "xa_w_kv", "xa_w_out", "ffn_w_gate_up", "ffn_w_down", "moe_w_gate_up", "moe_w_down")


def kernel(x_prompt, x_sample, mem_prompt, mem_sample, norm_mix, norm_xattn, norm_mem, norm_ffn, norm_final, na_w_qkv, na_rpb, na_w_out, lru_w_in, lru_conv_w, lru_conv_b, lru_gate_a_w, lru_gate_a_b, lru_gate_x_w, lru_gate_x_b, lru_lambda, lru_w_out, gla_w_in, gla_gate_w1, gla_gate_w2, gla_gate_b, gla_head_norm, gla_w_out, hy_w_in, hy_short_w, hy_short_b, hy_filt_w1, hy_filt_b1, hy_filt_w2, hy_filt_b2, hy_filt_w3, hy_filt_freq, hy_log_decay, hy_skip, hy_w_out, xa_w_q, xa_w_kv, xa_w_out, ffn_w_gate_up, ffn_w_down, moe_router, moe_w_gate_up, moe_w_down):
    p = dict(norm_mix=norm_mix, norm_xattn=norm_xattn, norm_mem=norm_mem, norm_ffn=norm_ffn, norm_final=norm_final,
             na_w_qkv=na_w_qkv, na_rpb=na_rpb, na_w_out=na_w_out,
             lru_w_in=lru_w_in, lru_conv_w=lru_conv_w, lru_conv_b=lru_conv_b, lru_gate_a_w=lru_gate_a_w,
             lru_gate_a_b=lru_gate_a_b, lru_gate_x_w=lru_gate_x_w, lru_gate_x_b=lru_gate_x_b, lru_lambda=lru_lambda,
             lru_w_out=lru_w_out,
             gla_w_in=gla_w_in, gla_gate_w1=gla_gate_w1, gla_gate_w2=gla_gate_w2, gla_gate_b=gla_gate_b,
             gla_head_norm=gla_head_norm, gla_w_out=gla_w_out,
             hy_w_in=hy_w_in, hy_short_w=hy_short_w, hy_short_b=hy_short_b, hy_filt_w1=hy_filt_w1,
             hy_filt_b1=hy_filt_b1, hy_filt_w2=hy_filt_w2, hy_filt_b2=hy_filt_b2, hy_filt_w3=hy_filt_w3,
             hy_filt_freq=hy_filt_freq, hy_log_decay=hy_log_decay, hy_skip=hy_skip, hy_w_out=hy_w_out,
             xa_w_q=xa_w_q, xa_w_kv=xa_w_kv, xa_w_out=xa_w_out,
             ffn_w_gate_up=ffn_w_gate_up, ffn_w_down=ffn_w_down,
             moe_router=moe_router, moe_w_gate_up=moe_w_gate_up, moe_w_down=moe_w_down)
    for name in _MATMUL_WEIGHTS:
        p[name] = p[name].astype(BF16)
    return (_trunk(x_prompt, mem_prompt, p), _trunk(x_sample, mem_sample, p))
```
